```python
import math
import jax, jax.numpy as jnp
from jax import lax
import numpy as np

D_MODEL = 1024
BATCH = 2
SEQ = 8192
DEPTH = 4
DEC_BATCH = 128
DEC_SEQ = 4
PAST_LEN = 8192
PAGE_SIZE = 128

HEAD_DIM = 64
N_Q_HEADS = 12
N_KV_HEADS = 4
GQA_GROUP = N_Q_HEADS // N_KV_HEADS
WINDOW = 128
ROT_DIM = HEAD_DIM // 4
ROPE_THETA = 500000.0
MEM_HEADS = 4
N_MEM = 256
SG_WIDTH = N_Q_HEADS * HEAD_DIM
SG_GROUPS = 4
SG_GROUP_DIM = SG_WIDTH // SG_GROUPS
CHUNK = 128
MIX_WIDTH = SG_WIDTH + MEM_HEADS * HEAD_DIM
A_IN = N_Q_HEADS * HEAD_DIM + 2 * N_KV_HEADS * HEAD_DIM + MEM_HEADS * HEAD_DIM
B_IN = 2 * SG_WIDTH + MEM_HEADS * HEAD_DIM
N_A_LAYERS = (DEPTH + 1) // 2
N_B_LAYERS = DEPTH // 2
PEER_HEADS = 8
N_KEYS = 128
N_EXPERTS = N_KEYS * N_KEYS
PEER_TOPK = 16
PEER_KEY_DIM = 256
PEER_HALF = PEER_KEY_DIM // 2
PEER_BLOCK = 256
DN_ALPHA = (2.0 * DEPTH) ** 0.25
DN_BETA = (8.0 * DEPTH) ** -0.25
LN_EPS = 1e-5
ATTN_SCALE = HEAD_DIM ** -0.5
NEG = -1e30

kernel_name = "hybrid_swa_sink_gmlp_peer_decoder_step"


def layer_norm(x, g, b):
    xf = x.astype(jnp.float32)
    mu = jnp.mean(xf, -1, keepdims=True)
    var = jnp.mean(jnp.square(xf - mu), -1, keepdims=True)
    return ((xf - mu) * lax.rsqrt(var + LN_EPS) * g.astype(jnp.float32) + b.astype(jnp.float32)).astype(x.dtype)


def post_norm(x, sub, g, b):
    return layer_norm(DN_ALPHA * x + sub, g, b)


def partial_rope(x, pos):
    half = ROT_DIM // 2
    inv = ROPE_THETA ** (-jnp.arange(half, dtype=jnp.float32) * (2.0 / ROT_DIM))
    ang = pos.astype(jnp.float32)[:, None] * inv[None, :]
    cos = jnp.cos(ang)[:, None, :]
    sin = jnp.sin(ang)[:, None, :]
    xr = x[..., :ROT_DIM].astype(jnp.float32)
    x1, x2 = xr[..., :half], xr[..., half:]
    rot = jnp.concatenate([x1 * cos - x2 * sin, x2 * cos + x1 * sin], -1).astype(x.dtype)
    return jnp.concatenate([rot, x[..., ROT_DIM:]], -1)


def sink_softmax(s, sink_col):
    full = jnp.concatenate([s, jnp.broadcast_to(sink_col, s.shape[:-1] + (1,))], -1)
    return jax.nn.softmax(full, axis=-1)[..., :-1]


def project_a(x, pos, w_in):
    n, s = x.shape[:2]
    z = x @ w_in
    qd = N_Q_HEADS * HEAD_DIM
    kd = N_KV_HEADS * HEAD_DIM
    q = partial_rope(z[..., :qd].reshape(n, s, N_Q_HEADS, HEAD_DIM), pos)
    k = partial_rope(z[..., qd:qd + kd].reshape(n, s, N_KV_HEADS, HEAD_DIM), pos)
    v = z[..., qd + kd:qd + 2 * kd].reshape(n, s, N_KV_HEADS, HEAD_DIM)
    qm = z[..., qd + 2 * kd:].reshape(n, s, MEM_HEADS, HEAD_DIM)
    return q, k, v, qm


def swa_prompt(q, k, v, sink):
    n, s = q.shape[:2]
    nb = s // WINDOW
    qb = q.reshape(n, nb, WINDOW, N_KV_HEADS, GQA_GROUP, HEAD_DIM)
    kb = k.reshape(n, nb, WINDOW, N_KV_HEADS, HEAD_DIM)
    vb = v.reshape(n, nb, WINDOW, N_KV_HEADS, HEAD_DIM)
    prev = lambda t: jnp.concatenate([jnp.zeros_like(t[:, :1]), t[:, :-1]], axis=1)
    k2 = jnp.concatenate([prev(kb), kb], axis=2)
    v2 = jnp.concatenate([prev(vb), vb], axis=2)
    sc = jnp.einsum('bnqhgd,bnkhd->bnhgqk', qb, k2, preferred_element_type=jnp.float32) * ATTN_SCALE
    qi = jnp.arange(WINDOW)[:, None] + WINDOW
    ki = jnp.arange(2 * WINDOW)[None, :]
    allowed = (ki <= qi) & (qi - ki < WINDOW)
    blk_ok = (jnp.arange(nb)[:, None, None] > 0) | (ki[None] >= WINDOW)
    mask = allowed[None] & blk_ok
    sc = jnp.where(mask[None, :, None, None], sc, NEG)
    p = sink_softmax(sc, sink.astype(jnp.float32).reshape(N_KV_HEADS, GQA_GROUP)[:, :, None, None])
    o = jnp.einsum('bnhgqk,bnkhd->bnqhgd', p.astype(v.dtype), v2)
    return o.reshape(n, s, N_Q_HEADS * HEAD_DIM)


def swa_sample(q, k_new, v_new, k_buf, v_buf, sink):
    n, t = q.shape[:2]
    wb = k_buf.shape[1]
    k_all = jnp.concatenate([k_buf, k_new], axis=1)
    v_all = jnp.concatenate([v_buf, v_new], axis=1)
    qg = q.reshape(n, t, N_KV_HEADS, GQA_GROUP, HEAD_DIM)
    sc = jnp.einsum('bqhgd,bkhd->bhgqk', qg, k_all, preferred_element_type=jnp.float32) * ATTN_SCALE
    q_pos = PAST_LEN + jnp.arange(t)
    k_pos = jnp.concatenate([PAST_LEN - wb + jnp.arange(wb), PAST_LEN + jnp.arange(t)])
    mask = (k_pos[None, :] <= q_pos[:, None]) & (q_pos[:, None] - k_pos[None, :] < WINDOW)
    sc = jnp.where(mask[None, None, None], sc, NEG)
    p = sink_softmax(sc, sink.astype(jnp.float32).reshape(N_KV_HEADS, GQA_GROUP)[:, :, None, None])
    o = jnp.einsum('bhgqk,bkhd->bqhgd', p.astype(v_all.dtype), v_all)
    return o.reshape(n, t, N_Q_HEADS * HEAD_DIM)


def spatial_gate(z, ln_g, ln_b, w_s, b_s, L):
    u = jax.nn.gelu(z[..., :SG_WIDTH], approximate=False)
    v = layer_norm(jax.nn.gelu(z[..., SG_WIDTH:], approximate=False), ln_g, ln_b)
    n, s = z.shape[:2]
    vc = v.reshape(n, s // L, L, SG_GROUPS, SG_GROUP_DIM)
    w = jnp.where(jnp.tril(jnp.ones((L, L), dtype=bool)), w_s[:, :L, :L], 0)
    sg = jnp.einsum('gts,bcsgd->bctgd', w, vc) + b_s[:, :L].T[:, :, None]
    return u * sg.reshape(n, s, SG_WIDTH), v


def mem_attend(q, mk, mv):
    n, s = q.shape[:2]
    sc = jnp.einsum('bqhd,bkhd->bhqk', q, mk, preferred_element_type=jnp.float32) * ATTN_SCALE
    p = jax.nn.softmax(sc, axis=-1)
    o = jnp.einsum('bhqk,bkhd->bqhd', p.astype(mv.dtype), mv)
    return o.reshape(n, s, MEM_HEADS * HEAD_DIM)


def peer(x, w_q, b_q, subkeys, u_tab, v_tab):
    shp = x.shape
    xt = x.reshape(-1, D_MODEL)
    ntok = xt.shape[0]
    nb = -(-ntok // PEER_BLOCK)
    xt = jnp.pad(xt, ((0, nb * PEER_BLOCK - ntok), (0, 0)))

    def block(xb):
        q = (xb @ w_q + b_q).reshape(-1, PEER_HEADS, 2, PEER_HALF)
        s = jnp.einsum('thcd,hcnd->thcn', q, subkeys, preferred_element_type=jnp.float32)
        sv, si = lax.top_k(s, PEER_TOPK)
        cand = (sv[:, :, 0, :, None] + sv[:, :, 1, None, :]).reshape(-1, PEER_HEADS, PEER_TOPK * PEER_TOPK)
        cid = (si[:, :, 0, :, None] * N_KEYS + si[:, :, 1, None, :]).reshape(-1, PEER_HEADS, PEER_TOPK * PEER_TOPK)
        fv, fi = lax.top_k(cand, PEER_TOPK)
        eid = jnp.take_along_axis(cid, fi, axis=-1)
        g = jax.nn.softmax(fv, axis=-1)
        ug = u_tab[eid]
        vg = v_tab[eid]
        h = jnp.einsum('td,thkd->thk', xb, ug, preferred_element_type=jnp.float32)
        a = (g * jax.nn.gelu(h, approximate=False)).astype(xb.dtype)
        return jnp.einsum('thk,thkd->td', a, vg)

    y = lax.map(block, xt.reshape(nb, PEER_BLOCK, D_MODEL))
    return y.reshape(-1, D_MODEL)[:ntok].reshape(shp)


def setup_inputs(seed: int = 0) -> dict:
    key = jax.random.key(seed)
    ks = jax.random.split(key, 28)
    f32 = jnp.float32
    nrm = lambda k, shape, scale: jax.random.normal(k, shape, f32) * scale
    win_buf = min(WINDOW, PAST_LEN)
    fan = D_MODEL ** -0.5
    a_cols = jnp.concatenate([
        jnp.full((N_Q_HEADS * HEAD_DIM + N_KV_HEADS * HEAD_DIM,), fan, f32),
        jnp.full((N_KV_HEADS * HEAD_DIM,), fan * DN_BETA, f32),
        jnp.full((MEM_HEADS * HEAD_DIM,), fan, f32)])
    mem_cols = jnp.concatenate([
        jnp.full((MEM_HEADS * HEAD_DIM,), fan, f32),
        jnp.full((MEM_HEADS * HEAD_DIM,), fan * DN_BETA, f32)])
    return {
        "x_prompt": nrm(ks[0], (BATCH, SEQ, D_MODEL), 1.0),
        "x_sample": nrm(ks[1], (DEC_BATCH, DEC_SEQ, D_MODEL), 1.0),
        "cache_swa_k": nrm(ks[2], (N_A_LAYERS, DEC_BATCH, win_buf, N_KV_HEADS, HEAD_DIM), 1.0),
        "cache_swa_v": nrm(ks[3], (N_A_LAYERS, DEC_BATCH, win_buf, N_KV_HEADS, HEAD_DIM), DN_BETA),
        "cache_mem_k": nrm(ks[4], (DEPTH, DEC_BATCH, N_MEM, MEM_HEADS, HEAD_DIM), 1.0),
        "cache_mem_v": nrm(ks[5], (DEPTH, DEC_BATCH, N_MEM, MEM_HEADS, HEAD_DIM), DN_BETA),
        "mem_prompt": nrm(ks[6], (BATCH, N_MEM, D_MODEL), 1.0),
        "a_w_in": nrm(ks[7], (N_A_LAYERS, D_MODEL, A_IN), 1.0) * a_cols,
        "a_sink": nrm(ks[8], (N_A_LAYERS, N_Q_HEADS), 1.0),
        "b_w_in": nrm(ks[9], (N_B_LAYERS, D_MODEL, B_IN), fan),
        "b_v_ln_g": 1.0 + nrm(ks[10], (N_B_LAYERS, SG_WIDTH), 0.02),
        "b_v_ln_b": nrm(ks[11], (N_B_LAYERS, SG_WIDTH), 0.02),
        "b_w_s": nrm(ks[12], (N_B_LAYERS, SG_GROUPS, CHUNK, CHUNK), CHUNK ** -0.5),
        "b_b_s": 1.0 + nrm(ks[13], (N_B_LAYERS, SG_GROUPS, CHUNK), 0.02),
        "w_mem_kv": nrm(ks[14], (DEPTH, D_MODEL, 2 * MEM_HEADS * HEAD_DIM), 1.0) * mem_cols,
        "w_out": nrm(ks[15], (DEPTH, MIX_WIDTH, D_MODEL), MIX_WIDTH ** -0.5 * DN_BETA),
        "ln1_g": 1.0 + nrm(ks[16], (DEPTH, D_MODEL), 0.02),
        "ln1_b": nrm(ks[17], (DEPTH, D_MODEL), 0.02),
        "ln2_g": 1.0 + nrm(ks[18], (DEPTH, D_MODEL), 0.02),
        "ln2_b": nrm(ks[19], (DEPTH, D_MODEL), 0.02),
        "peer_w_q": nrm(ks[20], (DEPTH, D_MODEL, PEER_HEADS * PEER_KEY_DIM), fan),
        "peer_b_q": nrm(ks[21], (DEPTH, PEER_HEADS * PEER_KEY_DIM), 0.01),
        "peer_subkeys": nrm(ks[22], (DEPTH, PEER_HEADS, 2, N_KEYS, PEER_HALF), PEER_HALF ** -0.5),
        "peer_u": nrm(ks[23], (DEPTH, N_EXPERTS, D_MODEL), fan),
        "peer_v": nrm(ks[24], (DEPTH, N_EXPERTS, D_MODEL), DN_BETA * PEER_HEADS ** -0.5),
    }


def reference(x_prompt, x_sample, cache_swa_k, cache_swa_v, cache_mem_k, cache_mem_v,
              mem_prompt, a_w_in, a_sink, b_w_in, b_v_ln_g, b_v_ln_b, b_w_s, b_b_s,
              w_mem_kv, w_out, ln1_g, ln1_b, ln2_g, ln2_b,
              peer_w_q, peer_b_q, peer_subkeys, peer_u, peer_v):
    n_p, seq = x_prompt.shape[:2]
    n_s, dec_seq = x_sample.shape[:2]
    pos_p = jnp.arange(seq, dtype=jnp.int32)
    pos_s = PAST_LEN + jnp.arange(dec_seq, dtype=jnp.int32)
    xp, xs = x_prompt, x_sample
    swa_kp, swa_vp, swa_ks, swa_vs, sg_vs, mem_kp, mem_vp = [], [], [], [], [], [], []
    for i in range(DEPTH):
        j = i // 2
        mkv = jnp.einsum('bmd,de->bme', mem_prompt, w_mem_kv[i]).reshape(n_p, N_MEM, 2, MEM_HEADS, HEAD_DIM)
        mk_p, mv_p = mkv[:, :, 0], mkv[:, :, 1]
        mem_kp.append(mk_p)
        mem_vp.append(mv_p)
        if i % 2 == 0:
            qp, kp, vp, qmp = project_a(xp, pos_p, a_w_in[j])
            qs, ks_, vs_, qms = project_a(xs, pos_s, a_w_in[j])
            tok_p = swa_prompt(qp, kp, vp, a_sink[j])
            tok_s = swa_sample(qs, ks_, vs_, cache_swa_k[j], cache_swa_v[j], a_sink[j])
            swa_kp.append(kp[:, -WINDOW:])
            swa_vp.append(vp[:, -WINDOW:])
            swa_ks.append(ks_)
            swa_vs.append(vs_)
        else:
            zp = xp @ b_w_in[j]
            zs = xs @ b_w_in[j]
            tok_p, _ = spatial_gate(zp[..., :2 * SG_WIDTH], b_v_ln_g[j], b_v_ln_b[j], b_w_s[j], b_b_s[j], CHUNK)
            tok_s, v_s = spatial_gate(zs[..., :2 * SG_WIDTH], b_v_ln_g[j], b_v_ln_b[j], b_w_s[j], b_b_s[j], dec_seq)
            sg_vs.append(v_s)
            qmp = zp[..., 2 * SG_WIDTH:].reshape(n_p, seq, MEM_HEADS, HEAD_DIM)
            qms = zs[..., 2 * SG_WIDTH:].reshape(n_s, dec_seq, MEM_HEADS, HEAD_DIM)
        mix_p = jnp.concatenate([tok_p, mem_attend(qmp, mk_p, mv_p)], axis=-1)
        mix_s = jnp.concatenate([tok_s, mem_attend(qms, cache_mem_k[i], cache_mem_v[i])], axis=-1)
        xp = post_norm(xp, mix_p @ w_out[i], ln1_g[i], ln1_b[i])
        xs = post_norm(xs, mix_s @ w_out[i], ln1_g[i], ln1_b[i])
        xp = post_norm(xp, peer(xp, peer_w_q[i], peer_b_q[i], peer_subkeys[i], peer_u[i], peer_v[i]), ln2_g[i], ln2_b[i])
        xs = post_norm(xs, peer(xs, peer_w_q[i], peer_b_q[i], peer_subkeys[i], peer_u[i], peer_v[i]), ln2_g[i], ln2_b[i])
    return (xp, xs, jnp.stack(swa_kp), jnp.stack(swa_vp), jnp.stack(swa_ks), jnp.stack(swa_vs),
            jnp.stack(sg_vs), jnp.stack(mem_kp), jnp.stack(mem_vp))
```

```python
import functools

import jax
import jax.numpy as jnp
from jax import lax
from jax.experimental import pallas as pl
from jax.experimental.pallas import tpu as pltpu

F32 = jnp.float32
BF16 = jnp.bfloat16

D_MODEL = 1024
HEAD_DIM = 64
HEAD_PAD = 128
N_Q_HEADS = 12
N_KV_HEADS = 4
GQA_GROUP = N_Q_HEADS // N_KV_HEADS
WINDOW = 128
ROT_DIM = HEAD_DIM // 4
ROT_HALF = ROT_DIM // 2
ROPE_THETA = 500000.0
MEM_HEADS = 4
SG_WIDTH = N_Q_HEADS * HEAD_DIM
SG_GROUPS = 4
SG_GROUP_DIM = SG_WIDTH // SG_GROUPS
SG_GROUP_PAD = 256
CHUNK = 128
PEER_HEADS = 8
N_KEYS = 128
PEER_TOPK = 16
PEER_HALF = 128
PAST_LEN = 8192
DEPTH = 4
DN_ALPHA = (2.0 * DEPTH) ** 0.25
LN_EPS = 1e-5
ATTN_SCALE = HEAD_DIM ** -0.5
NEG = -1e30

A_HEADS = N_Q_HEADS + 2 * N_KV_HEADS + MEM_HEADS
A_MIX = (N_Q_HEADS + MEM_HEADS) * HEAD_PAD
B_MIX = SG_GROUPS * SG_GROUP_PAD + MEM_HEADS * HEAD_PAD
SG_PAD = SG_GROUPS * SG_GROUP_PAD

TOKEN_TILE = 512
PEER_TOKEN_TILE = 512
PEER_EXPERT_TILE = 1024
SAMPLE_REQ_TILE = 16
VMEM_LIMIT = 56 * 1024 * 1024

NT_DIMS = (((1,), (1,)), ((), ()))


def _dot(a, b):
    return jnp.dot(a, b, preferred_element_type=F32)


def _dot_nt(a, b):
    return lax.dot_general(a, b, NT_DIMS, preferred_element_type=F32)


def _layer_norm(x, g, b):
    mu = jnp.mean(x, axis=-1, keepdims=True)
    xc = x - mu
    var = jnp.mean(xc * xc, axis=-1, keepdims=True)
    return xc * lax.rsqrt(var + LN_EPS) * g + b


def _gelu(x):
    return 0.5 * x * (1.0 + lax.erf(x * (0.5 ** 0.5)))


def _softmax_rows(s):
    m = jnp.max(s, axis=-1, keepdims=True)
    p = jnp.exp(s - m)
    return p / jnp.sum(p, axis=-1, keepdims=True)


def _params(*sem):
    return pltpu.CompilerParams(dimension_semantics=sem, vmem_limit_bytes=VMEM_LIMIT)


def _pad_last(w, groups, width, padded):
    lead = w.shape[:-1]
    w = w.reshape(lead + (groups, width))
    w = jnp.pad(w, [(0, 0)] * len(lead) + [(0, 0), (0, padded - width)])
    return w.reshape(lead + (groups * padded,))


def _pad_rows(w, groups, width, padded):
    return jnp.swapaxes(_pad_last(jnp.swapaxes(w, -1, -2), groups, width, padded), -1, -2)


def _unpad_last(x, groups, width, padded):
    lead = x.shape[:-1]
    return x.reshape(lead + (groups, padded))[..., :width]


def _memkv_kernel(m_ref, w_ref, o_ref):
    o_ref[...] = _dot(m_ref[...].astype(BF16), w_ref[...])


def _memkv(mem2d, w):
    depth = w.shape[0]
    rows = mem2d.shape[0]
    width = w.shape[-1]
    return pl.pallas_call(
        _memkv_kernel,
        out_shape=jax.ShapeDtypeStruct((depth, rows, width), F32),
        grid=(depth,),
        in_specs=[pl.BlockSpec((rows, D_MODEL), lambda i: (0, 0)),
                  pl.BlockSpec((None, D_MODEL, width), lambda i: (i, 0, 0))],
        out_specs=pl.BlockSpec((None, rows, width), lambda i: (i, 0, 0)),
        compiler_params=_params("arbitrary"),
        name="memkv",
    )(mem2d, w)


def _inproj_a_kernel(x_ref, w_ref, inv_ref, q_ref, k_ref, v_ref, qm_ref, *, tm, period, offset):
    i = pl.program_id(0)
    xb = x_ref[...].astype(BF16)
    row = i * tm + lax.broadcasted_iota(jnp.int32, (tm, 1), 0)
    pos = offset + lax.rem(row, period)
    ang = pos.astype(F32) * inv_ref[...]
    lane = lax.broadcasted_iota(jnp.int32, (1, HEAD_PAD), 1)
    cos = jnp.cos(ang)
    sin = jnp.sin(ang)
    c_mul = jnp.where(lane < ROT_DIM, cos, 1.0)
    s_lo = jnp.where(lane < ROT_HALF, -sin, 0.0)
    s_hi = jnp.where((lane >= ROT_HALF) & (lane < ROT_DIM), sin, 0.0)

    def rope(z):
        return (z * c_mul + pltpu.roll(z, HEAD_PAD - ROT_HALF, 1) * s_lo
                + pltpu.roll(z, ROT_HALF, 1) * s_hi)

    chunk_heads = 4
    for c in range(A_HEADS // chunk_heads):
        z = _dot(xb, w_ref[:, c * chunk_heads * HEAD_PAD:(c + 1) * chunk_heads * HEAD_PAD])
        for t in range(chunk_heads):
            head = c * chunk_heads + t
            zt = z[:, t * HEAD_PAD:(t + 1) * HEAD_PAD]
            if head < N_Q_HEADS:
                q_ref[:, head * HEAD_PAD:(head + 1) * HEAD_PAD] = rope(zt).astype(BF16)
            elif head < N_Q_HEADS + N_KV_HEADS:
                h = head - N_Q_HEADS
                k_ref[:, h * HEAD_PAD:(h + 1) * HEAD_PAD] = rope(zt)
            elif head < N_Q_HEADS + 2 * N_KV_HEADS:
                h = head - N_Q_HEADS - N_KV_HEADS
                v_ref[:, h * HEAD_PAD:(h + 1) * HEAD_PAD] = zt
            else:
                h = head - N_Q_HEADS - 2 * N_KV_HEADS
                qm_ref[:, h * HEAD_PAD:(h + 1) * HEAD_PAD] = zt.astype(BF16)


def _inproj_a(x, w, inv_row, period, offset):
    t = x.shape[0]
    tm = min(TOKEN_TILE, t)
    qw, kw = N_Q_HEADS * HEAD_PAD, N_KV_HEADS * HEAD_PAD
    mw = MEM_HEADS * HEAD_PAD
    row_blk = lambda width: pl.BlockSpec((tm, width), lambda i: (i, 0))
    return pl.pallas_call(
        functools.partial(_inproj_a_kernel, tm=tm, period=period, offset=offset),
        out_shape=(jax.ShapeDtypeStruct((t, qw), BF16), jax.ShapeDtypeStruct((t, kw), F32),
                   jax.ShapeDtypeStruct((t, kw), F32), jax.ShapeDtypeStruct((t, mw), BF16)),
        grid=(t // tm,),
        in_specs=[row_blk(D_MODEL),
                  pl.BlockSpec((D_MODEL, A_HEADS * HEAD_PAD), lambda i: (0, 0)),
                  pl.BlockSpec((1, HEAD_PAD), lambda i: (0, 0))],
        out_specs=(row_blk(qw), row_blk(kw), row_blk(kw), row_blk(mw)),
        compiler_params=_params("arbitrary"),
        name="inproj_a",
    )(x, w, inv_row)


def _inproj_b_kernel(x_ref, w_ref, g_ref, b_ref, u_ref, v_ref, qm_ref):
    xb = x_ref[...].astype(BF16)
    half = SG_PAD // 2
    for c in range(2):
        z = _dot(xb, w_ref[:, c * half:(c + 1) * half])
        u_ref[:, c * half:(c + 1) * half] = _gelu(z)
    gv = jnp.concatenate(
        [_gelu(_dot(xb, w_ref[:, SG_PAD + c * half:SG_PAD + (c + 1) * half])) for c in range(2)], axis=1)
    lane = lax.broadcasted_iota(jnp.int32, (1, SG_PAD), 1)
    real = (lane & (SG_GROUP_PAD - 1)) < SG_GROUP_DIM
    mu = jnp.sum(gv, axis=-1, keepdims=True) * (1.0 / SG_WIDTH)
    xc = jnp.where(real, gv - mu, 0.0)
    var = jnp.sum(xc * xc, axis=-1, keepdims=True) * (1.0 / SG_WIDTH)
    v_ref[...] = xc * lax.rsqrt(var + LN_EPS) * g_ref[...] + b_ref[...]
    qm_ref[...] = _dot(xb, w_ref[:, 2 * SG_PAD:]).astype(BF16)


def _inproj_b(x, w, g_row, b_row):
    t = x.shape[0]
    tm = min(TOKEN_TILE, t)
    mw = MEM_HEADS * HEAD_PAD
    row_blk = lambda width: pl.BlockSpec((tm, width), lambda i: (i, 0))
    full = lambda a: pl.BlockSpec(a.shape, lambda i: (0,) * a.ndim)
    return pl.pallas_call(
        _inproj_b_kernel,
        out_shape=(jax.ShapeDtypeStruct((t, SG_PAD), F32), jax.ShapeDtypeStruct((t, SG_PAD), F32),
                   jax.ShapeDtypeStruct((t, mw), BF16)),
        grid=(t // tm,),
        in_specs=[row_blk(D_MODEL), full(w), full(g_row), full(b_row)],
        out_specs=(row_blk(SG_PAD), row_blk(SG_PAD), row_blk(mw)),
        compiler_params=_params("arbitrary"),
        name="inproj_b",
    )(x, w, g_row, b_row)


def _mem_attend_shared(qm_ref, mk_ref, mv_ref, mix_ref, base):
    for h in range(MEM_HEADS):
        sl = slice(h * HEAD_PAD, (h + 1) * HEAD_PAD)
        s = _dot_nt(qm_ref[:, sl], mk_ref[:, sl].astype(BF16)) * ATTN_SCALE
        p = _softmax_rows(s)
        o = _dot(p.astype(BF16), mv_ref[:, sl].astype(BF16))
        mix_ref[:, base + h * HEAD_PAD:base + (h + 1) * HEAD_PAD] = o.astype(BF16)


def _out_norm(x_ref, mix_ref, wout_ref, g_ref, b_ref, o_ref):
    y = _dot(mix_ref[...], wout_ref[...])
    o_ref[...] = _layer_norm(DN_ALPHA * x_ref[...] + y, g_ref[...], b_ref[...])


def _mix_prompt_a_kernel(sink_ref, x_ref, q_ref, kc_ref, kp_ref, vc_ref, vp_ref, qm_ref, mk_ref, mv_ref,
                         wout_ref, g_ref, b_ref, o_ref, mix_ref, *, tq):
    i = pl.program_id(1)
    ri = lax.broadcasted_iota(jnp.int32, (WINDOW, 2 * WINDOW), 0)
    ci = lax.broadcasted_iota(jnp.int32, (WINDOW, 2 * WINDOW), 1)
    band = (ci > ri) & (ci <= ri + WINDOW)
    first_lim = jnp.where(i > 0, 0, WINDOW)
    for j in range(tq // WINDOW):
        rows = slice(j * WINDOW, (j + 1) * WINDOW)
        if j == 0:
            k_prev, v_prev = kp_ref[...], vp_ref[...]
            mask = band & (ci >= first_lim)
        else:
            prev = slice((j - 1) * WINDOW, j * WINDOW)
            k_prev, v_prev = kc_ref[prev, :], vc_ref[prev, :]
            mask = band
        k2 = jnp.concatenate([k_prev, kc_ref[rows, :]], axis=0).astype(BF16)
        v2 = jnp.concatenate([v_prev, vc_ref[rows, :]], axis=0).astype(BF16)
        mask3 = jnp.concatenate([mask] * GQA_GROUP, axis=0)
        for g in range(N_KV_HEADS):
            gs = slice(g * HEAD_PAD, (g + 1) * HEAD_PAD)
            heads = [g * GQA_GROUP + r for r in range(GQA_GROUP)]
            qg = jnp.concatenate([q_ref[rows, h * HEAD_PAD:(h + 1) * HEAD_PAD] for h in heads], axis=0)
            s = _dot_nt(qg, k2[:, gs]) * ATTN_SCALE
            s = jnp.where(mask3, s, NEG)
            sink = jnp.concatenate([jnp.full((WINDOW, 1), sink_ref[h], F32) for h in heads], axis=0)
            m = jnp.maximum(jnp.max(s, axis=-1, keepdims=True), sink)
            p = jnp.exp(s - m)
            den = jnp.sum(p, axis=-1, keepdims=True) + jnp.exp(sink - m)
            o = _dot((p / den).astype(BF16), v2[:, gs])
            for r, h in enumerate(heads):
                mix_ref[rows, h * HEAD_PAD:(h + 1) * HEAD_PAD] = o[r * WINDOW:(r + 1) * WINDOW].astype(BF16)
    _mem_attend_shared(qm_ref, mk_ref, mv_ref, mix_ref, N_Q_HEADS * HEAD_PAD)
    _out_norm(x_ref, mix_ref, wout_ref, g_ref, b_ref, o_ref)


def _mix_prompt_a(x, q, k, v, qm, memkv, layer, sink, wout, g_row, b_row, batch, seq):
    tq = TOKEN_TILE
    nqb = seq // tq
    wpb = tq // WINDOW
    kw = N_KV_HEADS * HEAD_PAD
    mw = MEM_HEADS * HEAD_PAD
    n_mem = memkv.shape[1] // batch
    cur = lambda width: pl.BlockSpec((tq, width), lambda b, i: (b * nqb + i, 0))
    prev = pl.BlockSpec((WINDOW, kw), lambda b, i: (b * (seq // WINDOW) + jnp.maximum(i * wpb - 1, 0), 0))
    const = lambda a: pl.BlockSpec(a.shape, lambda b, i: (0,) * a.ndim)
    mem = lambda part: pl.BlockSpec((None, n_mem, mw), lambda b, i: (layer, b, part))
    return pl.pallas_call(
        functools.partial(_mix_prompt_a_kernel, tq=tq),
        out_shape=jax.ShapeDtypeStruct(x.shape, F32),
        grid=(batch, nqb),
        in_specs=[pl.BlockSpec(memory_space=pltpu.SMEM),
                  cur(D_MODEL), cur(N_Q_HEADS * HEAD_PAD), cur(kw), prev, cur(kw), prev, cur(mw),
                  mem(0), mem(1), const(wout), const(g_row), const(b_row)],
        out_specs=cur(D_MODEL),
        scratch_shapes=[pltpu.VMEM((tq, A_MIX), BF16)],
        compiler_params=_params("arbitrary", "arbitrary"),
        name="mix_prompt_a",
    )(sink, x, q, k, k, v, v, qm, memkv, memkv, wout, g_row, b_row)


def _mix_prompt_b_kernel(x_ref, u_ref, v_ref, qm_ref, mk_ref, mv_ref, ws_ref, bs_ref,
                         wout_ref, g_ref, b_ref, o_ref, mix_ref, *, tq):
    ri = lax.broadcasted_iota(jnp.int32, (CHUNK, CHUNK), 0)
    ci = lax.broadcasted_iota(jnp.int32, (CHUNK, CHUNK), 1)
    causal = ci <= ri
    for g in range(SG_GROUPS):
        gs = slice(g * SG_GROUP_PAD, (g + 1) * SG_GROUP_PAD)
        w = jnp.where(causal, ws_ref[g], 0.0).astype(BF16)
        for c in range(tq // CHUNK):
            rows = slice(c * CHUNK, (c + 1) * CHUNK)
            sg = _dot(w, v_ref[rows, gs].astype(BF16)) + bs_ref[g]
            mix_ref[rows, gs] = (u_ref[rows, gs] * sg).astype(BF16)
    _mem_attend_shared(qm_ref, mk_ref, mv_ref, mix_ref, SG_PAD)
    _out_norm(x_ref, mix_ref, wout_ref, g_ref, b_ref, o_ref)


def _mix_prompt_b(x, u, v, qm, memkv, layer, ws, bs, wout, g_row, b_row, batch, seq):
    tq = TOKEN_TILE
    nqb = seq // tq
    mw = MEM_HEADS * HEAD_PAD
    n_mem = memkv.shape[1] // batch
    cur = lambda width: pl.BlockSpec((tq, width), lambda b, i: (b * nqb + i, 0))
    const = lambda a: pl.BlockSpec(a.shape, lambda b, i: (0,) * a.ndim)
    mem = lambda part: pl.BlockSpec((None, n_mem, mw), lambda b, i: (layer, b, part))
    return pl.pallas_call(
        functools.partial(_mix_prompt_b_kernel, tq=tq),
        out_shape=jax.ShapeDtypeStruct(x.shape, F32),
        grid=(batch, nqb),
        in_specs=[cur(D_MODEL), cur(SG_PAD), cur(SG_PAD), cur(mw), mem(0), mem(1),
                  const(ws), const(bs), const(wout), const(g_row), const(b_row)],
        out_specs=cur(D_MODEL),
        scratch_shapes=[pltpu.VMEM((tq, B_MIX), BF16)],
        compiler_params=_params("arbitrary", "arbitrary"),
        name="mix_prompt_b",
    )(x, u, v, qm, memkv, memkv, ws, bs, wout, g_row, b_row)


def _roll_half(x):
    return pltpu.roll(x.astype(F32), HEAD_DIM, 1)


def _mem_attend_cached(qm_ref, cmk_ref, cmv_ref, mix_ref, base, rows, dec_seq, n_mem):
    row_req = lax.broadcasted_iota(jnp.int32, (rows, 1), 0) // dec_seq
    col_req = lax.broadcasted_iota(jnp.int32, (1, cmk_ref.shape[0]), 1) // n_mem
    own = row_req == col_req
    for h in range(MEM_HEADS):
        pair = slice((h // 2) * HEAD_PAD, (h // 2 + 1) * HEAD_PAD)
        qh = qm_ref[:, h * HEAD_PAD:(h + 1) * HEAD_PAD]
        if h % 2:
            qh = _roll_half(qh).astype(BF16)
        s = _dot_nt(qh, cmk_ref[:, pair].astype(BF16)) * ATTN_SCALE
        p = _softmax_rows(jnp.where(own, s, NEG))
        o = _dot(p.astype(BF16), cmv_ref[:, pair].astype(BF16))
        if h % 2:
            o = pltpu.roll(o, HEAD_DIM, 1)
        mix_ref[:, base + h * HEAD_PAD:base + (h + 1) * HEAD_PAD] = o.astype(BF16)


def _mix_sample_a_kernel(sink_ref, x_ref, q_ref, kn_ref, vn_ref, ck_ref, cv_ref, qm_ref, cmk_ref, cmv_ref,
                         wout_ref, g_ref, b_ref, o_ref, mix_ref, *, rows, dec_seq, win_buf, n_mem):
    qrows = GQA_GROUP * rows
    rid = lax.rem(lax.broadcasted_iota(jnp.int32, (qrows, 1), 0), rows)
    row_req, row_t = rid // dec_seq, lax.rem(rid, dec_seq)
    cc = lax.broadcasted_iota(jnp.int32, (1, ck_ref.shape[0]), 1)
    mask_c = (row_req == cc // win_buf) & (row_t + win_buf - lax.rem(cc, win_buf) < WINDOW)
    cn = lax.broadcasted_iota(jnp.int32, (1, rows), 1)
    mask_n = (row_req == cn // dec_seq) & (lax.rem(cn, dec_seq) <= row_t)
    for g in range(N_KV_HEADS):
        gs = slice(g * HEAD_PAD, (g + 1) * HEAD_PAD)
        pair = slice((g // 2) * HEAD_PAD, (g // 2 + 1) * HEAD_PAD)
        heads = [g * GQA_GROUP + r for r in range(GQA_GROUP)]
        qg = jnp.concatenate([q_ref[:, h * HEAD_PAD:(h + 1) * HEAD_PAD] for h in heads], axis=0)
        qc = _roll_half(qg).astype(BF16) if g % 2 else qg
        s_c = jnp.where(mask_c, _dot_nt(qc, ck_ref[:, pair].astype(BF16)) * ATTN_SCALE, NEG)
        s_n = jnp.where(mask_n, _dot_nt(qg, kn_ref[:, gs].astype(BF16)) * ATTN_SCALE, NEG)
        sink = jnp.concatenate([jnp.full((rows, 1), sink_ref[h], F32) for h in heads], axis=0)
        m = jnp.maximum(jnp.maximum(jnp.max(s_c, axis=-1, keepdims=True),
                                    jnp.max(s_n, axis=-1, keepdims=True)), sink)
        p_c = jnp.exp(s_c - m)
        p_n = jnp.exp(s_n - m)
        den = (jnp.sum(p_c, axis=-1, keepdims=True) + jnp.sum(p_n, axis=-1, keepdims=True)
               + jnp.exp(sink - m))
        o_c = _dot((p_c / den).astype(BF16), cv_ref[:, pair].astype(BF16))
        if g % 2:
            o_c = pltpu.roll(o_c, HEAD_DIM, 1)
        o = o_c + _dot((p_n / den).astype(BF16), vn_ref[:, gs].astype(BF16))
        for r, h in enumerate(heads):
            mix_ref[:, h * HEAD_PAD:(h + 1) * HEAD_PAD] = o[r * rows:(r + 1) * rows].astype(BF16)
    _mem_attend_cached(qm_ref, cmk_ref, cmv_ref, mix_ref, N_Q_HEADS * HEAD_PAD, rows, dec_seq, n_mem)
    _out_norm(x_ref, mix_ref, wout_ref, g_ref, b_ref, o_ref)


def _mix_sample_a(x, q, kn, vn, ck, cv, qm, cmk, cmv, sink, wout, g_row, b_row, dec_seq, win_buf, n_mem):
    n_req = x.shape[0] // dec_seq
    rb = min(SAMPLE_REQ_TILE, n_req)
    rows = rb * dec_seq
    kw = N_KV_HEADS * HEAD_PAD
    mw = MEM_HEADS * HEAD_PAD
    cur = lambda width: pl.BlockSpec((rows, width), lambda i: (i, 0))
    const = lambda a: pl.BlockSpec(a.shape, lambda i: (0,) * a.ndim)
    cache = lambda per_req, a: pl.BlockSpec((rb * per_req, a.shape[1]), lambda i: (i, 0))
    return pl.pallas_call(
        functools.partial(_mix_sample_a_kernel, rows=rows, dec_seq=dec_seq, win_buf=win_buf, n_mem=n_mem),
        out_shape=jax.ShapeDtypeStruct(x.shape, F32),
        grid=(n_req // rb,),
        in_specs=[pl.BlockSpec(memory_space=pltpu.SMEM),
                  cur(D_MODEL), cur(N_Q_HEADS * HEAD_PAD), cur(kw), cur(kw),
                  cache(win_buf, ck), cache(win_buf, cv), cur(mw), cache(n_mem, cmk), cache(n_mem, cmv),
                  const(wout), const(g_row), const(b_row)],
        out_specs=cur(D_MODEL),
        scratch_shapes=[pltpu.VMEM((rows, A_MIX), BF16)],
        compiler_params=_params("arbitrary"),
        name="mix_sample_a",
    )(sink, x, q, kn, vn, ck, cv, qm, cmk, cmv, wout, g_row, b_row)


def _mix_sample_b_kernel(ws_ref, bs_ref, x_ref, u_ref, v_ref, qm_ref, cmk_ref, cmv_ref,
                         wout_ref, g_ref, b_ref, o_ref, mix_ref, *, rows, dec_seq, n_mem):
    ri = lax.broadcasted_iota(jnp.int32, (rows, rows), 0)
    ci = lax.broadcasted_iota(jnp.int32, (rows, rows), 1)
    same_req = (ri // dec_seq) == (ci // dec_seq)
    rt, ct = lax.rem(ri, dec_seq), lax.rem(ci, dec_seq)
    rt1 = lax.rem(lax.broadcasted_iota(jnp.int32, (rows, 1), 0), dec_seq)
    for g in range(SG_GROUPS):
        gs = slice(g * SG_GROUP_PAD, (g + 1) * SG_GROUP_PAD)
        w = jnp.zeros((rows, rows), F32)
        bias = jnp.zeros((rows, 1), F32)
        for a in range(dec_seq):
            bias = jnp.where(rt1 == a, bs_ref[g * dec_seq + a], bias)
            for b in range(a + 1):
                w = jnp.where(same_req & (rt == a) & (ct == b), ws_ref[(g * dec_seq + a) * dec_seq + b], w)
        sg = _dot(w.astype(BF16), v_ref[:, gs].astype(BF16)) + bias
        mix_ref[:, gs] = (u_ref[:, gs] * sg).astype(BF16)
    _mem_attend_cached(qm_ref, cmk_ref, cmv_ref, mix_ref, SG_PAD, rows, dec_seq, n_mem)
    _out_norm(x_ref, mix_ref, wout_ref, g_ref, b_ref, o_ref)


def _mix_sample_b(x, u, v, qm, cmk, cmv, ws4, bs4, wout, g_row, b_row, dec_seq, n_mem):
    n_req = x.shape[0] // dec_seq
    rb = min(SAMPLE_REQ_TILE, n_req)
    rows = rb * dec_seq
    mw = MEM_HEADS * HEAD_PAD
    cur = lambda width: pl.BlockSpec((rows, width), lambda i: (i, 0))
    const = lambda a: pl.BlockSpec(a.shape, lambda i: (0,) * a.ndim)
    cache = lambda per_req, a: pl.BlockSpec((rb * per_req, a.shape[1]), lambda i: (i, 0))
    smem = pl.BlockSpec(memory_space=pltpu.SMEM)
    return pl.pallas_call(
        functools.partial(_mix_sample_b_kernel, rows=rows, dec_seq=dec_seq, n_mem=n_mem),
        out_shape=jax.ShapeDtypeStruct(x.shape, F32),
        grid=(n_req // rb,),
        in_specs=[smem, smem, cur(D_MODEL), cur(SG_PAD), cur(SG_PAD), cur(mw),
                  cache(n_mem, cmk), cache(n_mem, cmv), const(wout), const(g_row), const(b_row)],
        out_specs=cur(D_MODEL),
        scratch_shapes=[pltpu.VMEM((rows, B_MIX), BF16)],
        compiler_params=_params("arbitrary"),
        name="mix_sample_b",
    )(ws4, bs4, x, u, v, qm, cmk, cmv, wout, g_row, b_row)


def _top_values(cur, n, store):
    for k in range(n):
        mk = jnp.max(cur, axis=0, keepdims=True)
        store(k, mk)
        if k + 1 < n:
            cur = jnp.where(cur == mk, -jnp.inf, cur)


def _peer_select_kernel(x_ref, wq_ref, bq_ref, keys_ref, e1_ref, thr_ref, e2_ref, s2_ref,
                        qt_ref, sv1_ref, sv2_ref, cand_ref, cv_ref):
    xb = x_ref[...].astype(BF16)
    qt_ref[...] = _dot_nt(wq_ref[...], xb) + bq_ref[...]

    def head(h, carry):
        base = pl.multiple_of(h * 2 * PEER_HALF, 2 * PEER_HALF)
        s1 = _dot(keys_ref[2 * h], qt_ref[pl.ds(base, PEER_HALF), :].astype(BF16))
        s2 = _dot(keys_ref[2 * h + 1], qt_ref[pl.ds(base + PEER_HALF, PEER_HALF), :].astype(BF16))

        def put(ref):
            def store(k, row):
                ref[k:k + 1, :] = row
            return store

        _top_values(s1, PEER_TOPK, put(sv1_ref))
        _top_values(s2, PEER_TOPK, put(sv2_ref))
        sv2 = sv2_ref[...]
        for a in range(PEER_TOPK):
            cand_ref[a * PEER_TOPK:(a + 1) * PEER_TOPK, :] = sv1_ref[a:a + 1, :] + sv2
        _top_values(cand_ref[...], PEER_TOPK + 1, put(cv_ref))
        top = cv_ref[0:1, :]
        z = jnp.sum(jnp.exp(cv_ref[0:PEER_TOPK, :] - top), axis=0, keepdims=True)
        tau = 0.5 * (cv_ref[PEER_TOPK - 1:PEER_TOPK, :] + cv_ref[PEER_TOPK:PEER_TOPK + 1, :])
        e1 = jnp.where(s1 >= sv1_ref[PEER_TOPK - 1:PEER_TOPK, :], jnp.exp(s1 - sv1_ref[0:1, :]), 0.0)
        e1_ref[h] = e1 / z
        e2_ref[h] = jnp.where(s2 >= sv2_ref[PEER_TOPK - 1:PEER_TOPK, :], jnp.exp(s2 - sv2_ref[0:1, :]), 0.0)
        thr_ref[h] = tau - s1
        s2_ref[h] = s2
        return carry

    lax.fori_loop(0, PEER_HEADS, head, 0)


def _peer_select(x, wq_t, bq_col, keys):
    t = x.shape[0]
    tt = min(PEER_TOKEN_TILE, t)
    qdim = wq_t.shape[0]
    sel = jax.ShapeDtypeStruct((PEER_HEADS, N_KEYS, t), F32)
    sel_blk = pl.BlockSpec((PEER_HEADS, N_KEYS, tt), lambda i: (0, 0, i))
    const = lambda a: pl.BlockSpec(a.shape, lambda i: (0,) * a.ndim)
    ncv = 24
    return pl.pallas_call(
        _peer_select_kernel,
        out_shape=(sel, sel, sel, sel),
        grid=(t // tt,),
        in_specs=[pl.BlockSpec((tt, D_MODEL), lambda i: (i, 0)), const(wq_t), const(bq_col), const(keys)],
        out_specs=(sel_blk, sel_blk, sel_blk, sel_blk),
        scratch_shapes=[pltpu.VMEM((qdim, tt), F32), pltpu.VMEM((PEER_TOPK, tt), F32),
                        pltpu.VMEM((PEER_TOPK, tt), F32), pltpu.VMEM((PEER_TOPK * PEER_TOPK, tt), F32),
                        pltpu.VMEM((ncv, tt), F32)],
        compiler_params=_params("arbitrary"),
        name="peer_select",
    )(x, wq_t, bq_col, keys)


def _peer_dense_kernel(x_ref, u_ref, vt_ref, e1_ref, thr_ref, e2_ref, s2_ref, g_ref, b_ref, o_ref,
                       xb_ref, ht_ref, p_ref, acc_ref, *, tt, eb):
    e = pl.program_id(1)

    @pl.when(e == 0)
    def _():
        xb_ref[...] = x_ref[...].astype(BF16)
        acc_ref[...] = jnp.zeros_like(acc_ref)

    ht_ref[...] = _dot_nt(u_ref[...], xb_ref[...])
    rows_per_step = eb // N_KEYS
    first_row = pl.multiple_of(e * rows_per_step, rows_per_step)
    for tl in range(tt // 128):
        lanes = slice(tl * 128, (tl + 1) * 128)
        thr_rows = [thr_ref[h, pl.ds(first_row, rows_per_step), lanes] for h in range(PEER_HEADS)]
        e1_rows = [e1_ref[h, pl.ds(first_row, rows_per_step), lanes] for h in range(PEER_HEADS)]
        for r in range(rows_per_step):
            a = jnp.zeros((N_KEYS, 128), F32)
            for h in range(PEER_HEADS):
                thr = thr_rows[h][r:r + 1, :]
                w1 = e1_rows[h][r:r + 1, :]
                a = a + jnp.where(s2_ref[h, :, lanes] >= thr, e2_ref[h, :, lanes], 0.0) * w1
            rows = slice(r * N_KEYS, (r + 1) * N_KEYS)
            p_ref[rows, lanes] = (a * _gelu(ht_ref[rows, lanes])).astype(BF16)
    acc_ref[...] += _dot(vt_ref[...], p_ref[...])

    @pl.when(e == pl.num_programs(1) - 1)
    def _():
        y = acc_ref[...].T
        o_ref[...] = _layer_norm(DN_ALPHA * x_ref[...] + y, g_ref[...], b_ref[...])


def _peer_dense(x, u, vt, e1, thr, e2, s2, g_row, b_row):
    t = x.shape[0]
    tt = min(PEER_TOKEN_TILE, t)
    n_exp = u.shape[0]
    eb = PEER_EXPERT_TILE
    sel_blk = pl.BlockSpec((PEER_HEADS, N_KEYS, tt), lambda i, e: (0, 0, i))
    const = lambda a: pl.BlockSpec(a.shape, lambda i, e: (0,) * a.ndim)
    return pl.pallas_call(
        functools.partial(_peer_dense_kernel, tt=tt, eb=eb),
        out_shape=jax.ShapeDtypeStruct(x.shape, F32),
        grid=(t // tt, n_exp // eb),
        in_specs=[pl.BlockSpec((tt, D_MODEL), lambda i, e: (i, 0)),
                  pl.BlockSpec((eb, D_MODEL), lambda i, e: (e, 0)),
                  pl.BlockSpec((D_MODEL, eb), lambda i, e: (0, e)),
                  sel_blk, sel_blk, sel_blk, sel_blk, const(g_row), const(b_row)],
        out_specs=pl.BlockSpec((tt, D_MODEL), lambda i, e: (i, 0)),
        scratch_shapes=[pltpu.VMEM((tt, D_MODEL), BF16), pltpu.VMEM((eb, tt), F32),
                        pltpu.VMEM((eb, tt), BF16), pltpu.VMEM((D_MODEL, tt), F32)],
        compiler_params=_params("arbitrary", "arbitrary"),
        name="peer_dense",
    )(x, u, vt, e1, thr, e2, s2, g_row, b_row)


def _peer(x, wq_t, bq_col, keys, u, vt, g_row, b_row):
    e1, thr, e2, s2 = _peer_select(x, wq_t, bq_col, keys)
    return _peer_dense(x, u, vt, e1, thr, e2, s2, g_row, b_row)


def kernel(x_prompt, x_sample, cache_swa_k, cache_swa_v, cache_mem_k, cache_mem_v, mem_prompt, a_w_in, a_sink, b_w_in, b_v_ln_g, b_v_ln_b, b_w_s, b_b_s, w_mem_kv, w_out, ln1_g, ln1_b, ln2_g, ln2_b, peer_w_q, peer_b_q, peer_subkeys, peer_u, peer_v):
    batch, seq, _ = x_prompt.shape
    n_req, dec_seq, _ = x_sample.shape
    depth = w_out.shape[0]
    n_mem = mem_prompt.shape[1]
    win_buf = cache_swa_k.shape[2]
    assert seq % TOKEN_TILE == 0 and (n_req * dec_seq) % 8 == 0 and depth == DEPTH

    xp = x_prompt.reshape(batch * seq, D_MODEL)
    xs = x_sample.reshape(n_req * dec_seq, D_MODEL)
    row = lambda a: a.reshape(1, -1)

    lane = jnp.arange(HEAD_PAD)
    inv = ROPE_THETA ** (-jnp.arange(ROT_HALF, dtype=F32) * (2.0 / ROT_DIM))
    inv_row = jnp.where(lane < ROT_DIM, inv[lane % ROT_HALF], 0.0).reshape(1, HEAD_PAD).astype(F32)
    w_in_a = _pad_last(a_w_in, A_HEADS, HEAD_DIM, HEAD_PAD).astype(BF16)
    bu = _pad_last(b_w_in[..., :SG_WIDTH], SG_GROUPS, SG_GROUP_DIM, SG_GROUP_PAD)
    bv = _pad_last(b_w_in[..., SG_WIDTH:2 * SG_WIDTH], SG_GROUPS, SG_GROUP_DIM, SG_GROUP_PAD)
    bm = _pad_last(b_w_in[..., 2 * SG_WIDTH:], MEM_HEADS, HEAD_DIM, HEAD_PAD)
    w_in_b = jnp.concatenate([bu, bv, bm], axis=-1).astype(BF16)
    ln_v_g = _pad_last(b_v_ln_g, SG_GROUPS, SG_GROUP_DIM, SG_GROUP_PAD)
    ln_v_b = _pad_last(b_v_ln_b, SG_GROUPS, SG_GROUP_DIM, SG_GROUP_PAD)
    w_memkv = _pad_last(w_mem_kv, 2 * MEM_HEADS, HEAD_DIM, HEAD_PAD).astype(BF16)
    w_out_mem = _pad_rows(w_out[:, SG_WIDTH:, :], MEM_HEADS, HEAD_DIM, HEAD_PAD)
    w_out_a = jnp.concatenate([_pad_rows(w_out[:, :SG_WIDTH, :], N_Q_HEADS, HEAD_DIM, HEAD_PAD), w_out_mem],
                              axis=1).astype(BF16)
    w_out_b = jnp.concatenate([_pad_rows(w_out[:, :SG_WIDTH, :], SG_GROUPS, SG_GROUP_DIM, SG_GROUP_PAD),
                               w_out_mem], axis=1).astype(BF16)
    bs_full = jnp.broadcast_to(b_b_s[..., None], b_b_s.shape + (SG_GROUP_PAD,))
    wq_t = jnp.swapaxes(peer_w_q, 1, 2).astype(BF16)
    bq_col = peer_b_q[..., None]
    keys = peer_subkeys.reshape(depth, PEER_HEADS * 2, N_KEYS, PEER_HALF).astype(BF16)
    u_tab = peer_u.astype(BF16)
    vt_tab = jnp.swapaxes(peer_v, 1, 2).astype(BF16)

    memkv = _memkv(mem_prompt.reshape(batch * n_mem, D_MODEL), w_memkv)
    mem_heads = memkv.reshape(depth, batch, n_mem, 2 * MEM_HEADS, HEAD_PAD)[..., :HEAD_DIM]
    mem_kp, mem_vp = mem_heads[..., :MEM_HEADS, :], mem_heads[..., MEM_HEADS:, :]

    ck2d = cache_swa_k.reshape(cache_swa_k.shape[0], n_req * win_buf, N_KV_HEADS * HEAD_DIM)
    cv2d = cache_swa_v.reshape(cache_swa_v.shape[0], n_req * win_buf, N_KV_HEADS * HEAD_DIM)
    cmk2d = cache_mem_k.reshape(depth, n_req * n_mem, MEM_HEADS * HEAD_DIM)
    cmv2d = cache_mem_v.reshape(depth, n_req * n_mem, MEM_HEADS * HEAD_DIM)

    swa_kp, swa_vp, swa_ks, swa_vs, sg_vs = [], [], [], [], []
    for i in range(depth):
        j = i // 2
        g1, b1, g2, b2 = row(ln1_g[i]), row(ln1_b[i]), row(ln2_g[i]), row(ln2_b[i])
        if i % 2 == 0:
            qp, kp, vp, qmp = _inproj_a(xp, w_in_a[j], inv_row, seq, 0)
            qs, ks, vs, qms = _inproj_a(xs, w_in_a[j], inv_row, dec_seq, PAST_LEN)
            xp = _mix_prompt_a(xp, qp, kp, vp, qmp, memkv, i, a_sink[j], w_out_a[i], g1, b1, batch, seq)
            xs = _mix_sample_a(xs, qs, ks, vs, ck2d[j], cv2d[j], qms, cmk2d[i], cmv2d[i], a_sink[j],
                               w_out_a[i], g1, b1, dec_seq, win_buf, n_mem)
            unpad = lambda a, n: _unpad_last(a, N_KV_HEADS, HEAD_DIM, HEAD_PAD).reshape(n, -1, N_KV_HEADS, HEAD_DIM)
            swa_kp.append(unpad(kp, batch)[:, -WINDOW:])
            swa_vp.append(unpad(vp, batch)[:, -WINDOW:])
            swa_ks.append(unpad(ks, n_req))
            swa_vs.append(unpad(vs, n_req))
        else:
            up, vvp, qmp = _inproj_b(xp, w_in_b[j], row(ln_v_g[j]), row(ln_v_b[j]))
            us, vvs, qms = _inproj_b(xs, w_in_b[j], row(ln_v_g[j]), row(ln_v_b[j]))
            xp = _mix_prompt_b(xp, up, vvp, qmp, memkv, i, b_w_s[j], bs_full[j], w_out_b[i], g1, b1, batch, seq)
            xs = _mix_sample_b(xs, us, vvs, qms, cmk2d[i], cmv2d[i], b_w_s[j][:, :dec_seq, :dec_seq].reshape(-1),
                               b_b_s[j][:, :dec_seq].reshape(-1), w_out_b[i], g1, b1, dec_seq, n_mem)
            sg_vs.append(_unpad_last(vvs, SG_GROUPS, SG_GROUP_DIM, SG_GROUP_PAD).reshape(n_req, dec_seq, SG_WIDTH))
        xp = _peer(xp, wq_t[i], bq_col[i], keys[i], u_tab[i], vt_tab[i], g2, b2)
        xs = _peer(xs, wq_t[i], bq_col[i], keys[i], u_tab[i], vt_tab[i], g2, b2)

    return (xp.reshape(batch, seq, D_MODEL), xs.reshape(n_req, dec_seq, D_MODEL),
            jnp.stack(swa_kp), jnp.stack(swa_vp), jnp.stack(swa_ks), jnp.stack(swa_vs),
            jnp.stack(sg_vs), mem_kp, mem_vp)
```

```python
import functools

import jax
import jax.numpy as jnp
from jax import lax
from jax.experimental import pallas as pl
from jax.experimental.pallas import tpu as pltpu

F32 = jnp.float32
BF16 = jnp.bfloat16

D_MODEL = 1024
HEAD_DIM = 64
HEAD_PAD = 128
N_Q_HEADS = 12
N_KV_HEADS = 4
GQA_GROUP = N_Q_HEADS // N_KV_HEADS
WINDOW = 128
ROT_DIM = HEAD_DIM // 4
ROT_HALF = ROT_DIM // 2
ROPE_THETA = 500000.0
MEM_HEADS = 4
SG_WIDTH = N_Q_HEADS * HEAD_DIM
SG_GROUPS = 4
SG_GROUP_DIM = SG_WIDTH // SG_GROUPS
SG_GROUP_PAD = 256
CHUNK = 128
PEER_HEADS = 8
N_KEYS = 128
PEER_TOPK = 16
PEER_HALF = 128
PAST_LEN = 8192
DEPTH = 4
DN_ALPHA = (2.0 * DEPTH) ** 0.25
LN_EPS = 1e-5
ATTN_SCALE = HEAD_DIM ** -0.5
NEG = -1e30

A_HEADS = N_Q_HEADS + 2 * N_KV_HEADS + MEM_HEADS
A_MIX = (N_Q_HEADS + MEM_HEADS) * HEAD_PAD
B_MIX = SG_GROUPS * SG_GROUP_PAD + MEM_HEADS * HEAD_PAD
SG_PAD = SG_GROUPS * SG_GROUP_PAD

TOKEN_TILE = 512
PEER_TOKEN_TILE = 512
PEER_EXPERT_TILE = 1024
PEER_LANE_CHUNK = 256
SAMPLE_REQ_TILE = 16
VMEM_LIMIT = 56 * 1024 * 1024

NT_DIMS = (((1,), (1,)), ((), ()))


def _dot(a, b):
    return jnp.dot(a, b, preferred_element_type=F32)


def _dot_nt(a, b):
    return lax.dot_general(a, b, NT_DIMS, preferred_element_type=F32)


def _layer_norm(x, g, b):
    mu = jnp.mean(x, axis=-1, keepdims=True)
    xc = x - mu
    var = jnp.mean(xc * xc, axis=-1, keepdims=True)
    return xc * lax.rsqrt(var + LN_EPS) * g + b


def _gelu(x):
    return 0.5 * x * (1.0 + lax.erf(x * (0.5 ** 0.5)))


def _softmax_rows(s):
    m = jnp.max(s, axis=-1, keepdims=True)
    p = jnp.exp(s - m)
    return p / jnp.sum(p, axis=-1, keepdims=True)


def _params(*sem):
    return pltpu.CompilerParams(dimension_semantics=sem, vmem_limit_bytes=VMEM_LIMIT)


def _pad_last(w, groups, width, padded):
    lead = w.shape[:-1]
    w = w.reshape(lead + (groups, width))
    w = jnp.pad(w, [(0, 0)] * len(lead) + [(0, 0), (0, padded - width)])
    return w.reshape(lead + (groups * padded,))


def _pad_rows(w, groups, width, padded):
    return jnp.swapaxes(_pad_last(jnp.swapaxes(w, -1, -2), groups, width, padded), -1, -2)


def _unpad_last(x, groups, width, padded):
    lead = x.shape[:-1]
    return x.reshape(lead + (groups, padded))[..., :width]


def _memkv_kernel(m_ref, w_ref, o_ref):
    o_ref[...] = _dot(m_ref[...].astype(BF16), w_ref[...])


def _memkv(mem2d, w):
    depth = w.shape[0]
    rows = mem2d.shape[0]
    width = w.shape[-1]
    return pl.pallas_call(
        _memkv_kernel,
        out_shape=jax.ShapeDtypeStruct((depth, rows, width), F32),
        grid=(depth,),
        in_specs=[pl.BlockSpec((rows, D_MODEL), lambda i: (0, 0)),
                  pl.BlockSpec((None, D_MODEL, width), lambda i: (i, 0, 0))],
        out_specs=pl.BlockSpec((None, rows, width), lambda i: (i, 0, 0)),
        compiler_params=_params("arbitrary"),
        name="memkv",
    )(mem2d, w)


def _inproj_a_kernel(x_ref, w_ref, inv_ref, q_ref, k_ref, v_ref, qm_ref, *, tm, period, offset):
    i = pl.program_id(0)
    xb = x_ref[...].astype(BF16)
    row = i * tm + lax.broadcasted_iota(jnp.int32, (tm, 1), 0)
    pos = offset + lax.rem(row, period)
    ang = pos.astype(F32) * inv_ref[...]
    lane = lax.broadcasted_iota(jnp.int32, (1, HEAD_PAD), 1)
    cos = jnp.cos(ang)
    sin = jnp.sin(ang)
    c_mul = jnp.where(lane < ROT_DIM, cos, 1.0)
    s_lo = jnp.where(lane < ROT_HALF, -sin, 0.0)
    s_hi = jnp.where((lane >= ROT_HALF) & (lane < ROT_DIM), sin, 0.0)

    def rope(z):
        return (z * c_mul + pltpu.roll(z, HEAD_PAD - ROT_HALF, 1) * s_lo
                + pltpu.roll(z, ROT_HALF, 1) * s_hi)

    chunk_heads = 4
    for c in range(A_HEADS // chunk_heads):
        z = _dot(xb, w_ref[:, c * chunk_heads * HEAD_PAD:(c + 1) * chunk_heads * HEAD_PAD])
        for t in range(chunk_heads):
            head = c * chunk_heads + t
            zt = z[:, t * HEAD_PAD:(t + 1) * HEAD_PAD]
            if head < N_Q_HEADS:
                q_ref[:, head * HEAD_PAD:(head + 1) * HEAD_PAD] = rope(zt).astype(BF16)
            elif head < N_Q_HEADS + N_KV_HEADS:
                h = head - N_Q_HEADS
                k_ref[:, h * HEAD_PAD:(h + 1) * HEAD_PAD] = rope(zt)
            elif head < N_Q_HEADS + 2 * N_KV_HEADS:
                h = head - N_Q_HEADS - N_KV_HEADS
                v_ref[:, h * HEAD_PAD:(h + 1) * HEAD_PAD] = zt
            else:
                h = head - N_Q_HEADS - 2 * N_KV_HEADS
                qm_ref[:, h * HEAD_PAD:(h + 1) * HEAD_PAD] = zt.astype(BF16)


def _inproj_a(x, w, inv_row, period, offset):
    t = x.shape[0]
    tm = min(TOKEN_TILE, t)
    qw, kw = N_Q_HEADS * HEAD_PAD, N_KV_HEADS * HEAD_PAD
    mw = MEM_HEADS * HEAD_PAD
    row_blk = lambda width: pl.BlockSpec((tm, width), lambda i: (i, 0))
    return pl.pallas_call(
        functools.partial(_inproj_a_kernel, tm=tm, period=period, offset=offset),
        out_shape=(jax.ShapeDtypeStruct((t, qw), BF16), jax.ShapeDtypeStruct((t, kw), F32),
                   jax.ShapeDtypeStruct((t, kw), F32), jax.ShapeDtypeStruct((t, mw), BF16)),
        grid=(t // tm,),
        in_specs=[row_blk(D_MODEL),
                  pl.BlockSpec((D_MODEL, A_HEADS * HEAD_PAD), lambda i: (0, 0)),
                  pl.BlockSpec((1, HEAD_PAD), lambda i: (0, 0))],
        out_specs=(row_blk(qw), row_blk(kw), row_blk(kw), row_blk(mw)),
        compiler_params=_params("arbitrary"),
        name="inproj_a",
    )(x, w, inv_row)


def _inproj_b_kernel(x_ref, w_ref, g_ref, b_ref, u_ref, v_ref, qm_ref):
    xb = x_ref[...].astype(BF16)
    half = SG_PAD // 2
    for c in range(2):
        z = _dot(xb, w_ref[:, c * half:(c + 1) * half])
        u_ref[:, c * half:(c + 1) * half] = _gelu(z)
    gv = jnp.concatenate(
        [_gelu(_dot(xb, w_ref[:, SG_PAD + c * half:SG_PAD + (c + 1) * half])) for c in range(2)], axis=1)
    lane = lax.broadcasted_iota(jnp.int32, (1, SG_PAD), 1)
    real = (lane & (SG_GROUP_PAD - 1)) < SG_GROUP_DIM
    mu = jnp.sum(gv, axis=-1, keepdims=True) * (1.0 / SG_WIDTH)
    xc = jnp.where(real, gv - mu, 0.0)
    var = jnp.sum(xc * xc, axis=-1, keepdims=True) * (1.0 / SG_WIDTH)
    v_ref[...] = xc * lax.rsqrt(var + LN_EPS) * g_ref[...] + b_ref[...]
    qm_ref[...] = _dot(xb, w_ref[:, 2 * SG_PAD:]).astype(BF16)


def _inproj_b(x, w, g_row, b_row):
    t = x.shape[0]
    tm = min(TOKEN_TILE, t)
    mw = MEM_HEADS * HEAD_PAD
    row_blk = lambda width: pl.BlockSpec((tm, width), lambda i: (i, 0))
    full = lambda a: pl.BlockSpec(a.shape, lambda i: (0,) * a.ndim)
    return pl.pallas_call(
        _inproj_b_kernel,
        out_shape=(jax.ShapeDtypeStruct((t, SG_PAD), F32), jax.ShapeDtypeStruct((t, SG_PAD), F32),
                   jax.ShapeDtypeStruct((t, mw), BF16)),
        grid=(t // tm,),
        in_specs=[row_blk(D_MODEL), full(w), full(g_row), full(b_row)],
        out_specs=(row_blk(SG_PAD), row_blk(SG_PAD), row_blk(mw)),
        compiler_params=_params("arbitrary"),
        name="inproj_b",
    )(x, w, g_row, b_row)


def _mem_attend_shared(qm_ref, mk_ref, mv_ref, mix_ref, base):
    for h in range(MEM_HEADS):
        sl = slice(h * HEAD_PAD, (h + 1) * HEAD_PAD)
        s = _dot_nt(qm_ref[:, sl], mk_ref[:, sl].astype(BF16)) * ATTN_SCALE
        p = _softmax_rows(s)
        o = _dot(p.astype(BF16), mv_ref[:, sl].astype(BF16))
        mix_ref[:, base + h * HEAD_PAD:base + (h + 1) * HEAD_PAD] = o.astype(BF16)


def _out_norm(x_ref, mix_ref, wout_ref, g_ref, b_ref, o_ref):
    y = _dot(mix_ref[...], wout_ref[...])
    o_ref[...] = _layer_norm(DN_ALPHA * x_ref[...] + y, g_ref[...], b_ref[...])


def _mix_prompt_a_kernel(sink_ref, x_ref, q_ref, kc_ref, kp_ref, vc_ref, vp_ref, qm_ref, mk_ref, mv_ref,
                         wout_ref, g_ref, b_ref, o_ref, mix_ref, *, tq):
    i = pl.program_id(1)
    ri = lax.broadcasted_iota(jnp.int32, (WINDOW, 2 * WINDOW), 0)
    ci = lax.broadcasted_iota(jnp.int32, (WINDOW, 2 * WINDOW), 1)
    band = (ci > ri) & (ci <= ri + WINDOW)
    first_lim = jnp.where(i > 0, 0, WINDOW)
    for j in range(tq // WINDOW):
        rows = slice(j * WINDOW, (j + 1) * WINDOW)
        if j == 0:
            k_prev, v_prev = kp_ref[...], vp_ref[...]
            mask = band & (ci >= first_lim)
        else:
            prev = slice((j - 1) * WINDOW, j * WINDOW)
            k_prev, v_prev = kc_ref[prev, :], vc_ref[prev, :]
            mask = band
        k2 = jnp.concatenate([k_prev, kc_ref[rows, :]], axis=0).astype(BF16)
        v2 = jnp.concatenate([v_prev, vc_ref[rows, :]], axis=0).astype(BF16)
        mask3 = jnp.concatenate([mask] * GQA_GROUP, axis=0)
        for g in range(N_KV_HEADS):
            gs = slice(g * HEAD_PAD, (g + 1) * HEAD_PAD)
            heads = [g * GQA_GROUP + r for r in range(GQA_GROUP)]
            qg = jnp.concatenate([q_ref[rows, h * HEAD_PAD:(h + 1) * HEAD_PAD] for h in heads], axis=0)
            s = _dot_nt(qg, k2[:, gs]) * ATTN_SCALE
            s = jnp.where(mask3, s, NEG)
            sink = jnp.concatenate([jnp.full((WINDOW, 1), sink_ref[h], F32) for h in heads], axis=0)
            m = jnp.maximum(jnp.max(s, axis=-1, keepdims=True), sink)
            p = jnp.exp(s - m)
            den = jnp.sum(p, axis=-1, keepdims=True) + jnp.exp(sink - m)
            o = _dot((p / den).astype(BF16), v2[:, gs])
            for r, h in enumerate(heads):
                mix_ref[rows, h * HEAD_PAD:(h + 1) * HEAD_PAD] = o[r * WINDOW:(r + 1) * WINDOW].astype(BF16)
    _mem_attend_shared(qm_ref, mk_ref, mv_ref, mix_ref, N_Q_HEADS * HEAD_PAD)
    _out_norm(x_ref, mix_ref, wout_ref, g_ref, b_ref, o_ref)


def _mix_prompt_a(x, q, k, v, qm, memkv, layer, sink, wout, g_row, b_row, batch, seq):
    tq = TOKEN_TILE
    nqb = seq // tq
    wpb = tq // WINDOW
    kw = N_KV_HEADS * HEAD_PAD
    mw = MEM_HEADS * HEAD_PAD
    n_mem = memkv.shape[1] // batch
    cur = lambda width: pl.BlockSpec((tq, width), lambda b, i: (b * nqb + i, 0))
    prev = pl.BlockSpec((WINDOW, kw), lambda b, i: (b * (seq // WINDOW) + jnp.maximum(i * wpb - 1, 0), 0))
    const = lambda a: pl.BlockSpec(a.shape, lambda b, i: (0,) * a.ndim)
    mem = lambda part: pl.BlockSpec((None, n_mem, mw), lambda b, i: (layer, b, part))
    return pl.pallas_call(
        functools.partial(_mix_prompt_a_kernel, tq=tq),
        out_shape=jax.ShapeDtypeStruct(x.shape, F32),
        grid=(batch, nqb),
        in_specs=[pl.BlockSpec(memory_space=pltpu.SMEM),
                  cur(D_MODEL), cur(N_Q_HEADS * HEAD_PAD), cur(kw), prev, cur(kw), prev, cur(mw),
                  mem(0), mem(1), const(wout), const(g_row), const(b_row)],
        out_specs=cur(D_MODEL),
        scratch_shapes=[pltpu.VMEM((tq, A_MIX), BF16)],
        compiler_params=_params("arbitrary", "arbitrary"),
        name="mix_prompt_a",
    )(sink, x, q, k, k, v, v, qm, memkv, memkv, wout, g_row, b_row)


def _mix_prompt_b_kernel(x_ref, u_ref, v_ref, qm_ref, mk_ref, mv_ref, ws_ref, bs_ref,
                         wout_ref, g_ref, b_ref, o_ref, mix_ref, *, tq):
    ri = lax.broadcasted_iota(jnp.int32, (CHUNK, CHUNK), 0)
    ci = lax.broadcasted_iota(jnp.int32, (CHUNK, CHUNK), 1)
    causal = ci <= ri
    for g in range(SG_GROUPS):
        gs = slice(g * SG_GROUP_PAD, (g + 1) * SG_GROUP_PAD)
        w = jnp.where(causal, ws_ref[g], 0.0).astype(BF16)
        for c in range(tq // CHUNK):
            rows = slice(c * CHUNK, (c + 1) * CHUNK)
            sg = _dot(w, v_ref[rows, gs].astype(BF16)) + bs_ref[g]
            mix_ref[rows, gs] = (u_ref[rows, gs] * sg).astype(BF16)
    _mem_attend_shared(qm_ref, mk_ref, mv_ref, mix_ref, SG_PAD)
    _out_norm(x_ref, mix_ref, wout_ref, g_ref, b_ref, o_ref)


def _mix_prompt_b(x, u, v, qm, memkv, layer, ws, bs, wout, g_row, b_row, batch, seq):
    tq = TOKEN_TILE
    nqb = seq // tq
    mw = MEM_HEADS * HEAD_PAD
    n_mem = memkv.shape[1] // batch
    cur = lambda width: pl.BlockSpec((tq, width), lambda b, i: (b * nqb + i, 0))
    const = lambda a: pl.BlockSpec(a.shape, lambda b, i: (0,) * a.ndim)
    mem = lambda part: pl.BlockSpec((None, n_mem, mw), lambda b, i: (layer, b, part))
    return pl.pallas_call(
        functools.partial(_mix_prompt_b_kernel, tq=tq),
        out_shape=jax.ShapeDtypeStruct(x.shape, F32),
        grid=(batch, nqb),
        in_specs=[cur(D_MODEL), cur(SG_PAD), cur(SG_PAD), cur(mw), mem(0), mem(1),
                  const(ws), const(bs), const(wout), const(g_row), const(b_row)],
        out_specs=cur(D_MODEL),
        scratch_shapes=[pltpu.VMEM((tq, B_MIX), BF16)],
        compiler_params=_params("arbitrary", "arbitrary"),
        name="mix_prompt_b",
    )(x, u, v, qm, memkv, memkv, ws, bs, wout, g_row, b_row)


def _roll_half(x):
    return pltpu.roll(x.astype(F32), HEAD_DIM, 1)


def _mem_attend_cached(qm_ref, cmk_ref, cmv_ref, mix_ref, base, rows, dec_seq, n_mem):
    row_req = lax.broadcasted_iota(jnp.int32, (rows, 1), 0) // dec_seq
    col_req = lax.broadcasted_iota(jnp.int32, (1, cmk_ref.shape[0]), 1) // n_mem
    own = row_req == col_req
    for h in range(MEM_HEADS):
        pair = slice((h // 2) * HEAD_PAD, (h // 2 + 1) * HEAD_PAD)
        qh = qm_ref[:, h * HEAD_PAD:(h + 1) * HEAD_PAD]
        if h % 2:
            qh = _roll_half(qh).astype(BF16)
        s = _dot_nt(qh, cmk_ref[:, pair].astype(BF16)) * ATTN_SCALE
        p = _softmax_rows(jnp.where(own, s, NEG))
        o = _dot(p.astype(BF16), cmv_ref[:, pair].astype(BF16))
        if h % 2:
            o = pltpu.roll(o, HEAD_DIM, 1)
        mix_ref[:, base + h * HEAD_PAD:base + (h + 1) * HEAD_PAD] = o.astype(BF16)


def _mix_sample_a_kernel(sink_ref, x_ref, q_ref, kn_ref, vn_ref, ck_ref, cv_ref, qm_ref, cmk_ref, cmv_ref,
                         wout_ref, g_ref, b_ref, o_ref, mix_ref, *, rows, dec_seq, win_buf, n_mem):
    qrows = GQA_GROUP * rows
    rid = lax.rem(lax.broadcasted_iota(jnp.int32, (qrows, 1), 0), rows)
    row_req, row_t = rid // dec_seq, lax.rem(rid, dec_seq)
    cc = lax.broadcasted_iota(jnp.int32, (1, ck_ref.shape[0]), 1)
    mask_c = (row_req == cc // win_buf) & (row_t + win_buf - lax.rem(cc, win_buf) < WINDOW)
    cn = lax.broadcasted_iota(jnp.int32, (1, rows), 1)
    mask_n = (row_req == cn // dec_seq) & (lax.rem(cn, dec_seq) <= row_t)
    for g in range(N_KV_HEADS):
        gs = slice(g * HEAD_PAD, (g + 1) * HEAD_PAD)
        pair = slice((g // 2) * HEAD_PAD, (g // 2 + 1) * HEAD_PAD)
        heads = [g * GQA_GROUP + r for r in range(GQA_GROUP)]
        qg = jnp.concatenate([q_ref[:, h * HEAD_PAD:(h + 1) * HEAD_PAD] for h in heads], axis=0)
        qc = _roll_half(qg).astype(BF16) if g % 2 else qg
        s_c = jnp.where(mask_c, _dot_nt(qc, ck_ref[:, pair].astype(BF16)) * ATTN_SCALE, NEG)
        s_n = jnp.where(mask_n, _dot_nt(qg, kn_ref[:, gs].astype(BF16)) * ATTN_SCALE, NEG)
        sink = jnp.concatenate([jnp.full((rows, 1), sink_ref[h], F32) for h in heads], axis=0)
        m = jnp.maximum(jnp.maximum(jnp.max(s_c, axis=-1, keepdims=True),
                                    jnp.max(s_n, axis=-1, keepdims=True)), sink)
        p_c = jnp.exp(s_c - m)
        p_n = jnp.exp(s_n - m)
        den = (jnp.sum(p_c, axis=-1, keepdims=True) + jnp.sum(p_n, axis=-1, keepdims=True)
               + jnp.exp(sink - m))
        o_c = _dot((p_c / den).astype(BF16), cv_ref[:, pair].astype(BF16))
        if g % 2:
            o_c = pltpu.roll(o_c, HEAD_DIM, 1)
        o = o_c + _dot((p_n / den).astype(BF16), vn_ref[:, gs].astype(BF16))
        for r, h in enumerate(heads):
            mix_ref[:, h * HEAD_PAD:(h + 1) * HEAD_PAD] = o[r * rows:(r + 1) * rows].astype(BF16)
    _mem_attend_cached(qm_ref, cmk_ref, cmv_ref, mix_ref, N_Q_HEADS * HEAD_PAD, rows, dec_seq, n_mem)
    _out_norm(x_ref, mix_ref, wout_ref, g_ref, b_ref, o_ref)


def _mix_sample_a(x, q, kn, vn, ck, cv, qm, cmk, cmv, sink, wout, g_row, b_row, dec_seq, win_buf, n_mem):
    n_req = x.shape[0] // dec_seq
    rb = min(SAMPLE_REQ_TILE, n_req)
    rows = rb * dec_seq
    kw = N_KV_HEADS * HEAD_PAD
    mw = MEM_HEADS * HEAD_PAD
    cur = lambda width: pl.BlockSpec((rows, width), lambda i: (i, 0))
    const = lambda a: pl.BlockSpec(a.shape, lambda i: (0,) * a.ndim)
    cache = lambda per_req, a: pl.BlockSpec((rb * per_req, a.shape[1]), lambda i: (i, 0))
    return pl.pallas_call(
        functools.partial(_mix_sample_a_kernel, rows=rows, dec_seq=dec_seq, win_buf=win_buf, n_mem=n_mem),
        out_shape=jax.ShapeDtypeStruct(x.shape, F32),
        grid=(n_req // rb,),
        in_specs=[pl.BlockSpec(memory_space=pltpu.SMEM),
                  cur(D_MODEL), cur(N_Q_HEADS * HEAD_PAD), cur(kw), cur(kw),
                  cache(win_buf, ck), cache(win_buf, cv), cur(mw), cache(n_mem, cmk), cache(n_mem, cmv),
                  const(wout), const(g_row), const(b_row)],
        out_specs=cur(D_MODEL),
        scratch_shapes=[pltpu.VMEM((rows, A_MIX), BF16)],
        compiler_params=_params("arbitrary"),
        name="mix_sample_a",
    )(sink, x, q, kn, vn, ck, cv, qm, cmk, cmv, wout, g_row, b_row)


def _mix_sample_b_kernel(ws_ref, bs_ref, x_ref, u_ref, v_ref, qm_ref, cmk_ref, cmv_ref,
                         wout_ref, g_ref, b_ref, o_ref, mix_ref, *, rows, dec_seq, n_mem):
    ri = lax.broadcasted_iota(jnp.int32, (rows, rows), 0)
    ci = lax.broadcasted_iota(jnp.int32, (rows, rows), 1)
    same_req = (ri // dec_seq) == (ci // dec_seq)
    rt, ct = lax.rem(ri, dec_seq), lax.rem(ci, dec_seq)
    rt1 = lax.rem(lax.broadcasted_iota(jnp.int32, (rows, 1), 0), dec_seq)
    for g in range(SG_GROUPS):
        gs = slice(g * SG_GROUP_PAD, (g + 1) * SG_GROUP_PAD)
        w = jnp.zeros((rows, rows), F32)
        bias = jnp.zeros((rows, 1), F32)
        for a in range(dec_seq):
            bias = jnp.where(rt1 == a, bs_ref[g * dec_seq + a], bias)
            for b in range(a + 1):
                w = jnp.where(same_req & (rt == a) & (ct == b), ws_ref[(g * dec_seq + a) * dec_seq + b], w)
        sg = _dot(w.astype(BF16), v_ref[:, gs].astype(BF16)) + bias
        mix_ref[:, gs] = (u_ref[:, gs] * sg).astype(BF16)
    _mem_attend_cached(qm_ref, cmk_ref, cmv_ref, mix_ref, SG_PAD, rows, dec_seq, n_mem)
    _out_norm(x_ref, mix_ref, wout_ref, g_ref, b_ref, o_ref)


def _mix_sample_b(x, u, v, qm, cmk, cmv, ws4, bs4, wout, g_row, b_row, dec_seq, n_mem):
    n_req = x.shape[0] // dec_seq
    rb = min(SAMPLE_REQ_TILE, n_req)
    rows = rb * dec_seq
    mw = MEM_HEADS * HEAD_PAD
    cur = lambda width: pl.BlockSpec((rows, width), lambda i: (i, 0))
    const = lambda a: pl.BlockSpec(a.shape, lambda i: (0,) * a.ndim)
    cache = lambda per_req, a: pl.BlockSpec((rb * per_req, a.shape[1]), lambda i: (i, 0))
    smem = pl.BlockSpec(memory_space=pltpu.SMEM)
    return pl.pallas_call(
        functools.partial(_mix_sample_b_kernel, rows=rows, dec_seq=dec_seq, n_mem=n_mem),
        out_shape=jax.ShapeDtypeStruct(x.shape, F32),
        grid=(n_req // rb,),
        in_specs=[smem, smem, cur(D_MODEL), cur(SG_PAD), cur(SG_PAD), cur(mw),
                  cache(n_mem, cmk), cache(n_mem, cmv), const(wout), const(g_row), const(b_row)],
        out_specs=cur(D_MODEL),
        scratch_shapes=[pltpu.VMEM((rows, B_MIX), BF16)],
        compiler_params=_params("arbitrary"),
        name="mix_sample_b",
    )(ws4, bs4, x, u, v, qm, cmk, cmv, wout, g_row, b_row)


_CAND_COLS = [min(PEER_TOPK, (PEER_TOPK + 1) // (a + 1)) for a in range(PEER_TOPK)]
assert _CAND_COLS[1] == 8 and max(_CAND_COLS[2:8]) <= 8 and set(_CAND_COLS[8:]) == {1}
_CAND_ROWS = 80
_EXTRACT_ROWS = -(-(PEER_TOPK + 1) // 8) * 8


def _doubled_bf16_words(x):
    u = lax.bitcast_convert_type(x.astype(BF16).astype(F32), jnp.uint32)
    return u | (u >> 16)


def _peer_rank_kernel(x_ref, wq_ref, bq_ref, keys_ref, w1_ref, cnt_ref, e2_ref, r2_ref,
                      qt_ref, sv1_ref, sv2_ref, cand_ref, cv_ref):
    xb = x_ref[...].astype(BF16)
    qt_ref[...] = _dot_nt(wq_ref[...], xb) + bq_ref[...]

    def head(h, carry):
        base = pl.multiple_of(h * 2 * PEER_HALF, 2 * PEER_HALF)
        s1 = _dot(keys_ref[2 * h], qt_ref[pl.ds(base, PEER_HALF), :].astype(BF16))
        s2 = _dot(keys_ref[2 * h + 1], qt_ref[pl.ds(base + PEER_HALF, PEER_HALF), :].astype(BF16))

        cur = s1
        for k in range(PEER_TOPK):
            mk = jnp.max(cur, axis=0, keepdims=True)
            sv1_ref[k:k + 1, :] = mk
            if k + 1 < PEER_TOPK:
                cur = jnp.where(cur == mk, -jnp.inf, cur)
        cur = s2
        r2 = jnp.full(s2.shape, float(PEER_TOPK), F32)
        for k in range(PEER_TOPK):
            mk = jnp.max(cur, axis=0, keepdims=True)
            sv2_ref[k:k + 1, :] = mk
            hit = cur == mk
            r2 = jnp.where(hit, float(k), r2)
            cur = jnp.where(hit, -jnp.inf, cur)

        sub = lax.broadcasted_iota(jnp.int32, (8, 1), 0)
        cand_ref[0:16, :] = sv1_ref[0:1, :] + sv2_ref[...]
        for a in range(1, 8):
            sums = sv1_ref[a:a + 1, :] + sv2_ref[0:8, :]
            cand_ref[8 + 8 * a:16 + 8 * a, :] = jnp.where(sub < _CAND_COLS[a], sums, -jnp.inf)
        cand_ref[72:80, :] = sv1_ref[8:16, :] + sv2_ref[0:1, :]
        cur = cand_ref[...]
        for k in range(PEER_TOPK + 1):
            mk = jnp.max(cur, axis=0, keepdims=True)
            cv_ref[k:k + 1, :] = mk
            if k < PEER_TOPK:
                cur = jnp.where(cur == mk, -jnp.inf, cur)
        top = cv_ref[0:1, :]
        z = jnp.sum(jnp.exp(cv_ref[0:PEER_TOPK, :] - top), axis=0, keepdims=True)
        tau = 0.5 * (cv_ref[PEER_TOPK - 1:PEER_TOPK, :] + cv_ref[PEER_TOPK:PEER_TOPK + 1, :])

        in_top1 = s1 >= sv1_ref[PEER_TOPK - 1:PEER_TOPK, :]
        cnt = jnp.zeros(s1.shape, F32)
        for b in range(PEER_TOPK):
            cnt = cnt + jnp.where(s1 + sv2_ref[b:b + 1, :] >= tau, 1.0, 0.0)
        cnt_w = _doubled_bf16_words(jnp.where(in_top1, cnt, 0.0))
        w1_w = _doubled_bf16_words(jnp.where(in_top1, jnp.exp(s1 - sv1_ref[0:1, :]), 0.0) / z)
        e2 = jnp.where(r2 < PEER_TOPK, jnp.exp(s2 - sv2_ref[0:1, :]), 0.0)
        e2_w = pltpu.bitcast(e2.astype(BF16), jnp.uint32)
        r2_w = pltpu.bitcast(r2.astype(BF16), jnp.uint32)
        for c in range(s1.shape[1] // PEER_LANE_CHUNK):
            lanes = slice(c * PEER_LANE_CHUNK, (c + 1) * PEER_LANE_CHUNK)
            cnt_ref[c, h] = cnt_w[:, lanes]
            w1_ref[c, h] = w1_w[:, lanes]
            e2_ref[c, h] = e2_w[:, lanes]
            r2_ref[c, h] = r2_w[:, lanes]
        return carry

    lax.fori_loop(0, PEER_HEADS, head, 0)


def _peer_rank(x, wq_t, bq_col, keys):
    t = x.shape[0]
    tt = min(PEER_TOKEN_TILE, t)
    assert t % tt == 0 and tt % PEER_LANE_CHUNK == 0
    nc = tt // PEER_LANE_CHUNK
    qdim = wq_t.shape[0]
    sel = lambda rows: jax.ShapeDtypeStruct((t // PEER_LANE_CHUNK, PEER_HEADS, rows, PEER_LANE_CHUNK), jnp.uint32)
    blk = lambda rows: pl.BlockSpec((nc, PEER_HEADS, rows, PEER_LANE_CHUNK), lambda i: (i, 0, 0, 0))
    const = lambda a: pl.BlockSpec(a.shape, lambda i: (0,) * a.ndim)
    return pl.pallas_call(
        _peer_rank_kernel,
        out_shape=(sel(N_KEYS), sel(N_KEYS), sel(N_KEYS // 2), sel(N_KEYS // 2)),
        grid=(t // tt,),
        in_specs=[pl.BlockSpec((tt, D_MODEL), lambda i: (i, 0)), const(wq_t), const(bq_col), const(keys)],
        out_specs=(blk(N_KEYS), blk(N_KEYS), blk(N_KEYS // 2), blk(N_KEYS // 2)),
        scratch_shapes=[pltpu.VMEM((qdim, tt), F32), pltpu.VMEM((PEER_TOPK, tt), F32),
                        pltpu.VMEM((PEER_TOPK, tt), F32), pltpu.VMEM((_CAND_ROWS, tt), F32),
                        pltpu.VMEM((_EXTRACT_ROWS, tt), F32)],
        compiler_params=_params("arbitrary"),
        name="peer_rank",
    )(x, wq_t, bq_col, keys)


def _peer_expert_kernel(x_ref, u_ref, vt_ref, w1_ref, cnt_ref, e2_ref, r2_ref, g_ref, b_ref, o_ref,
                        xb_ref, p_ref, acc_ref, *, tt, eb):
    e = pl.program_id(1)
    slot = lax.rem(e, 2)
    n_chunks = tt // PEER_LANE_CHUNK
    v_rows = D_MODEL // n_chunks

    @pl.when(e == 0)
    def _():
        xb_ref[...] = x_ref[...].astype(BF16)
        acc_ref[...] = jnp.zeros_like(acc_ref)
        p_ref[1] = jnp.zeros(p_ref.shape[1:], jnp.uint32)

    rows_per_step = eb // N_KEYS
    first_row = pl.multiple_of(jnp.minimum(e, pl.num_programs(1) - 2) * rows_per_step, rows_per_step)
    zero = jnp.zeros((N_KEYS, 128), BF16)

    def packed_row(words, r):
        return pltpu.bitcast(jnp.broadcast_to(words[r:r + 1, :], (N_KEYS // 2, 128)), BF16)

    def chunk(c, carry):
        t0 = pl.multiple_of(c * PEER_LANE_CHUNK, PEER_LANE_CHUNK)
        ht = _dot_nt(u_ref[...], xb_ref[pl.ds(t0, PEER_LANE_CHUNK), :])
        p_prev = jnp.concatenate([pltpu.bitcast(p_ref[1 - slot, k], BF16) for k in range(n_chunks)], axis=1)
        v0 = pl.multiple_of(c * v_rows, v_rows)
        acc_ref[pl.ds(v0, v_rows), :] += _dot(vt_ref[pl.ds(v0, v_rows), :], p_prev)
        for tl in range(PEER_LANE_CHUNK // 128):
            lanes = slice(tl * 128, (tl + 1) * 128)
            for r in range(rows_per_step):
                a = zero
                for h in range(PEER_HEADS):
                    cnt = packed_row(cnt_ref[c, h, pl.ds(first_row, rows_per_step), lanes], r)
                    w1 = packed_row(w1_ref[c, h, pl.ds(first_row, rows_per_step), lanes], r)
                    keep = pltpu.bitcast(r2_ref[c, h, :, lanes], BF16) < cnt
                    a = a + jnp.where(keep, pltpu.bitcast(e2_ref[c, h, :, lanes], BF16), zero) * w1
                g = _gelu(ht[r * N_KEYS:(r + 1) * N_KEYS, lanes]).astype(BF16)
                p_ref[slot, c, r * (N_KEYS // 2):(r + 1) * (N_KEYS // 2), lanes] = pltpu.bitcast(a * g, jnp.uint32)
        return carry

    lax.fori_loop(0, n_chunks, chunk, 0)

    @pl.when(e == pl.num_programs(1) - 1)
    def _():
        y = acc_ref[...].T
        o_ref[...] = _layer_norm(DN_ALPHA * x_ref[...] + y, g_ref[...], b_ref[...])


def _peer_expert(x, u, vt, w1, cnt, e2, r2, g_row, b_row):
    t = x.shape[0]
    tt = min(PEER_TOKEN_TILE, t)
    nc = tt // PEER_LANE_CHUNK
    n_exp = u.shape[0]
    eb = PEER_EXPERT_TILE
    n_blk = n_exp // eb
    blk = lambda rows: pl.BlockSpec((nc, PEER_HEADS, rows, PEER_LANE_CHUNK), lambda i, e: (i, 0, 0, 0))
    const = lambda a: pl.BlockSpec(a.shape, lambda i, e: (0,) * a.ndim)
    return pl.pallas_call(
        functools.partial(_peer_expert_kernel, tt=tt, eb=eb),
        out_shape=jax.ShapeDtypeStruct(x.shape, F32),
        grid=(t // tt, n_blk + 1),
        in_specs=[pl.BlockSpec((tt, D_MODEL), lambda i, e: (i, 0)),
                  pl.BlockSpec((eb, D_MODEL), lambda i, e: (jnp.minimum(e, n_blk - 1), 0)),
                  pl.BlockSpec((D_MODEL, eb), lambda i, e: (0, jnp.maximum(e - 1, 0))),
                  blk(N_KEYS), blk(N_KEYS), blk(N_KEYS // 2), blk(N_KEYS // 2), const(g_row), const(b_row)],
        out_specs=pl.BlockSpec((tt, D_MODEL), lambda i, e: (i, 0)),
        scratch_shapes=[pltpu.VMEM((tt, D_MODEL), BF16),
                        pltpu.VMEM((2, nc, eb // 2, PEER_LANE_CHUNK), jnp.uint32),
                        pltpu.VMEM((D_MODEL, tt), F32)],
        compiler_params=_params("arbitrary", "arbitrary"),
        name="peer_expert",
    )(x, u, vt, w1, cnt, e2, r2, g_row, b_row)


def _peer(x, wq_t, bq_col, keys, u, vt, g_row, b_row):
    w1, cnt, e2, r2 = _peer_rank(x, wq_t, bq_col, keys)
    return _peer_expert(x, u, vt, w1, cnt, e2, r2, g_row, b_row)


def kernel(x_prompt, x_sample, cache_swa_k, cache_swa_v, cache_mem_k, cache_mem_v, mem_prompt, a_w_in, a_sink, b_w_in, b_v_ln_g, b_v_ln_b, b_w_s, b_b_s, w_mem_kv, w_out, ln1_g, ln1_b, ln2_g, ln2_b, peer_w_q, peer_b_q, peer_subkeys, peer_u, peer_v):
    batch, seq, _ = x_prompt.shape
    n_req, dec_seq, _ = x_sample.shape
    depth = w_out.shape[0]
    n_mem = mem_prompt.shape[1]
    win_buf = cache_swa_k.shape[2]
    assert seq % TOKEN_TILE == 0 and (n_req * dec_seq) % 8 == 0 and depth == DEPTH

    xp = x_prompt.reshape(batch * seq, D_MODEL)
    xs = x_sample.reshape(n_req * dec_seq, D_MODEL)
    row = lambda a: a.reshape(1, -1)

    lane = jnp.arange(HEAD_PAD)
    inv = ROPE_THETA ** (-jnp.arange(ROT_HALF, dtype=F32) * (2.0 / ROT_DIM))
    inv_row = jnp.where(lane < ROT_DIM, inv[lane % ROT_HALF], 0.0).reshape(1, HEAD_PAD).astype(F32)
    w_in_a = _pad_last(a_w_in, A_HEADS, HEAD_DIM, HEAD_PAD).astype(BF16)
    bu = _pad_last(b_w_in[..., :SG_WIDTH], SG_GROUPS, SG_GROUP_DIM, SG_GROUP_PAD)
    bv = _pad_last(b_w_in[..., SG_WIDTH:2 * SG_WIDTH], SG_GROUPS, SG_GROUP_DIM, SG_GROUP_PAD)
    bm = _pad_last(b_w_in[..., 2 * SG_WIDTH:], MEM_HEADS, HEAD_DIM, HEAD_PAD)
    w_in_b = jnp.concatenate([bu, bv, bm], axis=-1).astype(BF16)
    ln_v_g = _pad_last(b_v_ln_g, SG_GROUPS, SG_GROUP_DIM, SG_GROUP_PAD)
    ln_v_b = _pad_last(b_v_ln_b, SG_GROUPS, SG_GROUP_DIM, SG_GROUP_PAD)
    w_memkv = _pad_last(w_mem_kv, 2 * MEM_HEADS, HEAD_DIM, HEAD_PAD).astype(BF16)
    w_out_mem = _pad_rows(w_out[:, SG_WIDTH:, :], MEM_HEADS, HEAD_DIM, HEAD_PAD)
    w_out_a = jnp.concatenate([_pad_rows(w_out[:, :SG_WIDTH, :], N_Q_HEADS, HEAD_DIM, HEAD_PAD), w_out_mem],
                              axis=1).astype(BF16)
    w_out_b = jnp.concatenate([_pad_rows(w_out[:, :SG_WIDTH, :], SG_GROUPS, SG_GROUP_DIM, SG_GROUP_PAD),
                               w_out_mem], axis=1).astype(BF16)
    bs_full = jnp.broadcast_to(b_b_s[..., None], b_b_s.shape + (SG_GROUP_PAD,))
    wq_t = jnp.swapaxes(peer_w_q, 1, 2).astype(BF16)
    bq_col = peer_b_q[..., None]
    keys = peer_subkeys.reshape(depth, PEER_HEADS * 2, N_KEYS, PEER_HALF).astype(BF16)
    u_tab = peer_u.astype(BF16)
    vt_tab = jnp.swapaxes(peer_v, 1, 2).astype(BF16)

    memkv = _memkv(mem_prompt.reshape(batch * n_mem, D_MODEL), w_memkv)
    mem_heads = memkv.reshape(depth, batch, n_mem, 2 * MEM_HEADS, HEAD_PAD)[..., :HEAD_DIM]
    mem_kp, mem_vp = mem_heads[..., :MEM_HEADS, :], mem_heads[..., MEM_HEADS:, :]

    ck2d = cache_swa_k.reshape(cache_swa_k.shape[0], n_req * win_buf, N_KV_HEADS * HEAD_DIM)
    cv2d = cache_swa_v.reshape(cache_swa_v.shape[0], n_req * win_buf, N_KV_HEADS * HEAD_DIM)
    cmk2d = cache_mem_k.reshape(depth, n_req * n_mem, MEM_HEADS * HEAD_DIM)
    cmv2d = cache_mem_v.reshape(depth, n_req * n_mem, MEM_HEADS * HEAD_DIM)

    swa_kp, swa_vp, swa_ks, swa_vs, sg_vs = [], [], [], [], []
    for i in range(depth):
        j = i // 2
        g1, b1, g2, b2 = row(ln1_g[i]), row(ln1_b[i]), row(ln2_g[i]), row(ln2_b[i])
        if i % 2 == 0:
            qp, kp, vp, qmp = _inproj_a(xp, w_in_a[j], inv_row, seq, 0)
            qs, ks, vs, qms = _inproj_a(xs, w_in_a[j], inv_row, dec_seq, PAST_LEN)
            xp = _mix_prompt_a(xp, qp, kp, vp, qmp, memkv, i, a_sink[j], w_out_a[i], g1, b1, batch, seq)
            xs = _mix_sample_a(xs, qs, ks, vs, ck2d[j], cv2d[j], qms, cmk2d[i], cmv2d[i], a_sink[j],
                               w_out_a[i], g1, b1, dec_seq, win_buf, n_mem)
            unpad = lambda a, n: _unpad_last(a, N_KV_HEADS, HEAD_DIM, HEAD_PAD).reshape(n, -1, N_KV_HEADS, HEAD_DIM)
            swa_kp.append(unpad(kp, batch)[:, -WINDOW:])
            swa_vp.append(unpad(vp, batch)[:, -WINDOW:])
            swa_ks.append(unpad(ks, n_req))
            swa_vs.append(unpad(vs, n_req))
        else:
            up, vvp, qmp = _inproj_b(xp, w_in_b[j], row(ln_v_g[j]), row(ln_v_b[j]))
            us, vvs, qms = _inproj_b(xs, w_in_b[j], row(ln_v_g[j]), row(ln_v_b[j]))
            xp = _mix_prompt_b(xp, up, vvp, qmp, memkv, i, b_w_s[j], bs_full[j], w_out_b[i], g1, b1, batch, seq)
            xs = _mix_sample_b(xs, us, vvs, qms, cmk2d[i], cmv2d[i], b_w_s[j][:, :dec_seq, :dec_seq].reshape(-1),
                               b_b_s[j][:, :dec_seq].reshape(-1), w_out_b[i], g1, b1, dec_seq, n_mem)
            sg_vs.append(_unpad_last(vvs, SG_GROUPS, SG_GROUP_DIM, SG_GROUP_PAD).reshape(n_req, dec_seq, SG_WIDTH))
        xp = _peer(xp, wq_t[i], bq_col[i], keys[i], u_tab[i], vt_tab[i], g2, b2)
        xs = _peer(xs, wq_t[i], bq_col[i], keys[i], u_tab[i], vt_tab[i], g2, b2)

    return (xp.reshape(batch, seq, D_MODEL), xs.reshape(n_req, dec_seq, D_MODEL),
            jnp.stack(swa_kp), jnp.stack(swa_vp), jnp.stack(swa_ks), jnp.stack(swa_vs),
            jnp.stack(sg_vs), mem_kp, mem_vp)
```
